```python
import jax
import jax.numpy as jnp
from jax import lax
import numpy as np

D_MODEL = 4096
BATCH = 2
SEQ = 8192
DEPTH = 2

HEAD_DIM = 128
MIX_HEADS = D_MODEL // HEAD_DIM
QBLOCK = 128
ROPE_THETA = 10000.0
LN_EPS = 1e-5
RMS_EPS = 1e-6

SB_HEADS = MIX_HEADS // 4
DIL_HEADS = MIX_HEADS // 4
DIL_BRANCHES = ((128, 1), (512, 4), (2048, 16))
MLA_HEADS = MIX_HEADS // 4
MLA_Q_RANK = D_MODEL // 4
MLA_KV_RANK = 512
MLA_NOPE = 128
MLA_ROPE = 64
MLA_V = 128
DIFF_HEADS = MIX_HEADS // 8
DIFF_DIM = HEAD_DIM
D_FF = 4 * D_MODEL
PLE_DIM = 256
DN_ALPHA = (2 * DEPTH) ** 0.25
DN_BETA = (8 * DEPTH) ** -0.25

IN_WIDTHS = (SB_HEADS * HEAD_DIM,) * 3 + (DIL_HEADS * HEAD_DIM,) * 3 + (MLA_Q_RANK, MLA_KV_RANK, MLA_ROPE) + (DIFF_HEADS * 2 * DIFF_DIM,) * 3
N_IN = sum(IN_WIDTHS)
MIX_OUT = SB_HEADS * HEAD_DIM + DIL_HEADS * HEAD_DIM + MLA_HEADS * MLA_V + DIFF_HEADS * 2 * DIFF_DIM

kernel_name = 'hybrid_parallel_heads_deepnorm'


def layer_norm(x, g, b):
    xf = x.astype(jnp.float32)
    mu = jnp.mean(xf, axis=-1, keepdims=True)
    var = jnp.mean(jnp.square(xf - mu), axis=-1, keepdims=True)
    return ((xf - mu) * lax.rsqrt(var + LN_EPS) * g + b).astype(x.dtype)


def rms_norm(x, g):
    xf = x.astype(jnp.float32)
    return (xf * lax.rsqrt(jnp.mean(jnp.square(xf), axis=-1, keepdims=True) + RMS_EPS) * g).astype(x.dtype)


def rope(x, pos):
    half = x.shape[-1] // 2
    inv = ROPE_THETA ** (-jnp.arange(half, dtype=jnp.float32) / half)
    ang = pos.astype(jnp.float32)[:, None] * inv[None, :]
    cos, sin = jnp.cos(ang), jnp.sin(ang)
    x1 = x[..., :half].astype(jnp.float32)
    x2 = x[..., half:].astype(jnp.float32)
    return jnp.concatenate([x1 * cos - x2 * sin, x2 * cos + x1 * sin], axis=-1).astype(x.dtype)


def _heads(t, n_heads):
    B, S, _ = t.shape
    return t.reshape(B, S, n_heads, -1).transpose(0, 2, 1, 3)


def _merge(t):
    B, H, S, d = t.shape
    return t.transpose(0, 2, 1, 3).reshape(B, S, H * d)


def over_query_blocks(block_fn, *qs):
    B, H, S, _ = qs[0].shape
    nb = S // QBLOCK
    blks = tuple(q.reshape(B, H, nb, QBLOCK, q.shape[-1]).transpose(2, 0, 1, 3, 4) for q in qs)
    starts = jnp.arange(nb, dtype=jnp.int32) * QBLOCK
    out = lax.map(lambda a: block_fn(a[0] + jnp.arange(QBLOCK, dtype=jnp.int32), *a[1:]), (starts,) + blks)
    return out.transpose(1, 2, 0, 3, 4).reshape(B, H, S, out.shape[-1])


def stick_breaking_attention(q, k, v):
    S = k.shape[2]
    kpos = jnp.arange(S, dtype=jnp.int32)
    scale = HEAD_DIM ** -0.5

    def block(qpos, qb):
        z = jnp.einsum('bhqd,bhkd->bhqk', qb, k).astype(jnp.float32) * scale
        past = kpos[None, :] < qpos[:, None]
        log_beta = jax.nn.log_sigmoid(z)
        log_1m = jnp.where(past, log_beta - z, 0.0)
        between = lax.cumsum(log_1m, axis=3, reverse=True) - log_1m
        w = jnp.where(past, jnp.exp(log_beta + between), 0.0)
        return jnp.einsum('bhqk,bhkd->bhqd', w.astype(v.dtype), v)

    return over_query_blocks(block, q)


def _dilated_branch(q, k, v, span, dil):
    B, H, S, dh = q.shape
    L = S // dil
    nb = -(-L // QBLOCK)
    Lp = nb * QBLOCK

    def strided(t):
        t = t.reshape(B, H, L, dil, dh).transpose(0, 1, 3, 2, 4)
        t = jnp.pad(t, ((0, 0), (0, 0), (0, 0), (0, Lp - L), (0, 0)))
        return t.reshape(B, H, dil, nb, QBLOCK, dh)

    def with_prev(t):
        prev = jnp.pad(t[:, :, :, :-1], ((0, 0), (0, 0), (0, 0), (1, 0), (0, 0), (0, 0)))
        return jnp.concatenate([prev, t], axis=4)

    qs = strided(q)
    kk = with_prev(strided(k))
    vv = with_prev(strided(v))
    s = jnp.einsum('bhrnqd,bhrnkd->bhrnqk', qs, kk).astype(jnp.float32) * HEAD_DIM ** -0.5
    dist = jnp.arange(QBLOCK)[:, None] + QBLOCK - jnp.arange(2 * QBLOCK)[None, :]
    in_band = (dist >= 0) & (dist <= span)
    missing = (jnp.arange(nb)[:, None, None] == 0) & (jnp.arange(2 * QBLOCK) < QBLOCK)[None, None, :]
    valid = in_band[None] & jnp.logical_not(missing)
    s = jnp.where(valid, s, -jnp.inf)
    m = jnp.max(s, axis=-1, keepdims=True)
    e = jnp.exp(s - m)
    den = jnp.sum(e, axis=-1)
    o = jnp.einsum('bhrnqk,bhrnkd->bhrnqd', e, vv.astype(jnp.float32)) / den[..., None]
    lse = m[..., 0] + jnp.log(den)
    o = o.reshape(B, H, dil, Lp, dh)[:, :, :, :L].transpose(0, 1, 3, 2, 4).reshape(B, H, S, dh)
    lse = lse.reshape(B, H, dil, Lp)[..., :L].transpose(0, 1, 3, 2).reshape(B, H, S)
    return o, lse


def dilated_attention(q, k, v):
    outs, lses = [], []
    for window, dil in DIL_BRANCHES:
        o, lse = _dilated_branch(q, k, v, window // dil, dil)
        outs.append(o)
        lses.append(lse)
    w = jax.nn.softmax(jnp.stack(lses), axis=0)
    return jnp.sum(w[..., None] * jnp.stack(outs), axis=0).astype(q.dtype)


def causal_softmax_attention(q, k, v, scale):
    kpos = jnp.arange(k.shape[2], dtype=jnp.int32)

    def block(qpos, qb):
        s = jnp.einsum('bhqd,bhkd->bhqk', qb, k).astype(jnp.float32) * scale
        p = jax.nn.softmax(jnp.where(kpos[None, :] <= qpos[:, None], s, -jnp.inf), axis=-1)
        return jnp.einsum('bhqk,bhkd->bhqd', p.astype(v.dtype), v)

    return over_query_blocks(block, q)


def mla(c_q, c_kv, k_rope_in, pos, q_norm_g, w_uq, kv_norm_g, w_ukv):
    B, S, _ = c_q.shape
    q = _heads(rms_norm(c_q, q_norm_g) @ w_uq, MLA_HEADS)
    q = jnp.concatenate([q[..., :MLA_NOPE], rope(q[..., MLA_NOPE:], pos)], axis=-1)
    kv = _heads(rms_norm(c_kv, kv_norm_g) @ w_ukv, MLA_HEADS)
    k_nope, v = kv[..., :MLA_NOPE], kv[..., MLA_NOPE:]
    k_rope = jnp.broadcast_to(rope(k_rope_in, pos)[:, None], (B, MLA_HEADS, S, MLA_ROPE))
    k = jnp.concatenate([k_nope, k_rope], axis=-1)
    return causal_softmax_attention(q, k, v, (MLA_NOPE + MLA_ROPE) ** -0.5)


def diff_attention(q, k, v, pos, lam_params, subln_g, lam_init):
    B, S, _ = q.shape

    def pair(t):
        t = t.reshape(B, S, DIFF_HEADS, 2, DIFF_DIM).transpose(0, 3, 2, 1, 4)
        return rope(t[:, 0], pos), rope(t[:, 1], pos)

    q1, q2 = pair(q)
    k1, k2 = pair(k)
    vh = _heads(v, DIFF_HEADS)
    lp = lam_params.astype(jnp.float32)
    lam = jnp.exp(jnp.sum(lp[0] * lp[1])) - jnp.exp(jnp.sum(lp[2] * lp[3])) + lam_init
    scale = DIFF_DIM ** -0.5
    kpos = jnp.arange(S, dtype=jnp.int32)

    def block(qpos, q1b, q2b):
        causal = kpos[None, :] <= qpos[:, None]

        def probs(qb, kb):
            s = jnp.einsum('bhqd,bhkd->bhqk', qb, kb).astype(jnp.float32) * scale
            return jax.nn.softmax(jnp.where(causal, s, -jnp.inf), axis=-1)

        a = probs(q1b, k1) - lam * probs(q2b, k2)
        return jnp.einsum('bhqk,bhkd->bhqd', a.astype(vh.dtype), vh)

    o = over_query_blocks(block, q1, q2)
    return rms_norm(o, subln_g) * (1.0 - lam_init)


def hybrid_layer(x, p_i, layer_idx, w_in, w_o, mla_q_norm, mla_w_uq, mla_kv_norm, mla_w_ukv,
                 diff_lambda, diff_subln, ln_attn_g, ln_attn_b, w_ff1, w_ff2, ln_ff_g, ln_ff_b,
                 w_ple_gate, w_ple_proj, ln_ple_g, ln_ple_b):
    S = x.shape[1]
    pos = jnp.arange(S, dtype=jnp.int32)
    h = x @ w_in
    cuts, acc = [], 0
    for wdt in IN_WIDTHS[:-1]:
        acc += wdt
        cuts.append(acc)
    (a_q, a_k, a_v, b_q, b_k, b_v, c_q, c_kv, c_kr, d_q, d_k, d_v) = jnp.split(h, cuts, axis=-1)

    y_a = stick_breaking_attention(_heads(a_q, SB_HEADS), _heads(a_k, SB_HEADS), _heads(a_v, SB_HEADS))
    y_b = dilated_attention(rope(_heads(b_q, DIL_HEADS), pos), rope(_heads(b_k, DIL_HEADS), pos), _heads(b_v, DIL_HEADS))
    y_c = mla(c_q, c_kv, c_kr, pos, mla_q_norm, mla_w_uq, mla_kv_norm, mla_w_ukv)
    lam_init = 0.8 - 0.6 * float(np.exp(-0.3 * layer_idx))
    y_d = diff_attention(d_q, d_k, d_v, pos, diff_lambda, diff_subln, lam_init)

    mix = jnp.concatenate([_merge(y_a), _merge(y_b), _merge(y_c), _merge(y_d)], axis=-1) @ w_o
    x = layer_norm(DN_ALPHA * x + mix, ln_attn_g, ln_attn_b)

    f = jnp.square(jax.nn.relu(x @ w_ff1)) @ w_ff2
    x = layer_norm(DN_ALPHA * x + f, ln_ff_g, ln_ff_b)

    e = jax.nn.sigmoid(x @ w_ple_gate) * (p_i @ w_ple_proj)
    return layer_norm(DN_ALPHA * x + e, ln_ple_g, ln_ple_b)


def setup_inputs(seed: int = 0) -> dict:
    key = jax.random.key(seed)
    ks = jax.random.split(key, 20)
    f32 = jnp.float32

    def nrm(k, shape, scale):
        return jax.random.normal(k, shape, f32) * scale

    def gain(k, shape):
        return 1.0 + 0.02 * jax.random.normal(k, shape, f32)

    L = DEPTH
    return {
        'x': nrm(ks[0], (BATCH, SEQ, D_MODEL), 1.0),
        'p': nrm(ks[1], (DEPTH, BATCH, SEQ, PLE_DIM), 1.0),
        'w_in': nrm(ks[2], (L, D_MODEL, N_IN), D_MODEL ** -0.5),
        'w_o': nrm(ks[3], (L, MIX_OUT, D_MODEL), DN_BETA * MIX_OUT ** -0.5),
        'mla_q_norm': gain(ks[4], (L, MLA_Q_RANK)),
        'mla_w_uq': nrm(ks[5], (L, MLA_Q_RANK, MLA_HEADS * (MLA_NOPE + MLA_ROPE)), MLA_Q_RANK ** -0.5),
        'mla_kv_norm': gain(ks[6], (L, MLA_KV_RANK)),
        'mla_w_ukv': nrm(ks[7], (L, MLA_KV_RANK, MLA_HEADS * (MLA_NOPE + MLA_V)), MLA_KV_RANK ** -0.5),
        'diff_lambda': nrm(ks[8], (L, 4, DIFF_DIM), 0.1),
        'diff_subln': gain(ks[9], (L, 2 * DIFF_DIM)),
        'ln_attn_g': gain(ks[10], (L, D_MODEL)),
        'ln_attn_b': nrm(ks[11], (L, D_MODEL), 0.02),
        'w_ff1': nrm(ks[12], (L, D_MODEL, D_FF), D_MODEL ** -0.5),
        'w_ff2': nrm(ks[13], (L, D_FF, D_MODEL), DN_BETA * D_FF ** -0.5),
        'ln_ff_g': gain(ks[14], (L, D_MODEL)),
        'ln_ff_b': nrm(ks[15], (L, D_MODEL), 0.02),
        'w_ple_gate': nrm(ks[16], (L, D_MODEL, D_MODEL), D_MODEL ** -0.5),
        'w_ple_proj': nrm(ks[17], (L, PLE_DIM, D_MODEL), DN_BETA * PLE_DIM ** -0.5),
        'ln_ple_g': gain(ks[18], (L, D_MODEL)),
        'ln_ple_b': nrm(ks[19], (L, D_MODEL), 0.02),
    }


def reference(x, p, w_in, w_o, mla_q_norm, mla_w_uq, mla_kv_norm, mla_w_ukv, diff_lambda, diff_subln,
              ln_attn_g, ln_attn_b, w_ff1, w_ff2, ln_ff_g, ln_ff_b, w_ple_gate, w_ple_proj, ln_ple_g, ln_ple_b):
    for i in range(DEPTH):
        x = hybrid_layer(x, p[i], i, w_in[i], w_o[i], mla_q_norm[i], mla_w_uq[i], mla_kv_norm[i], mla_w_ukv[i],
                         diff_lambda[i], diff_subln[i], ln_attn_g[i], ln_attn_b[i], w_ff1[i], w_ff2[i],
                         ln_ff_g[i], ln_ff_b[i], w_ple_gate[i], w_ple_proj[i], ln_ple_g[i], ln_ple_b[i])
    return x
```

```python
import functools
import math

import jax
import jax.numpy as jnp
from jax import lax
from jax.experimental import pallas as pl
from jax.experimental.pallas import tpu as pltpu

F32 = jnp.float32
BF16 = jnp.bfloat16

LANE = 128
HEAD_DIM = 128
QBLOCK = 128
ROPE_THETA = 10000.0
LN_EPS = 1e-5
RMS_EPS = 1e-6
MLA_NOPE = 128
MLA_ROPE = 64
MLA_V = 128
MLA_QK_PAD = 256
DIL_BRANCHES = ((128, 1), (512, 4), (2048, 16))
DIL_CHUNK = QBLOCK * 16
VMEM_LIMIT_BYTES = 56 * 1024 * 1024


def _pick(n, cands):
    for c in cands:
        if n % c == 0:
            return c
    return n


def _cparams(sem):
    return pltpu.CompilerParams(dimension_semantics=sem, vmem_limit_bytes=VMEM_LIMIT_BYTES)


def _dot(a, b):
    return jnp.dot(a, b, preferred_element_type=F32)


def _dot_nt(a, b):
    return lax.dot_general(a, b, (((1,), (1,)), ((), ())), preferred_element_type=F32)


def _mm_kernel(*refs, act, rope):
    if rope:
        x_ref, w_ref, cos_ref, sin_ref, o_ref = refs
    else:
        x_ref, w_ref, o_ref = refs
    acc = _dot(x_ref[...], w_ref[...])
    if act == "relu2":
        acc = jnp.square(jnp.maximum(acc, 0.0))
    if rope:
        cos = cos_ref[...]
        sin = sin_ref[...]
        for c in range(acc.shape[1] // LANE):
            blk = acc[:, c * LANE:(c + 1) * LANE]
            rot = pltpu.roll(blk, LANE // 2, 1)
            o_ref[:, c * LANE:(c + 1) * LANE] = (blk * cos + rot * sin).astype(o_ref.dtype)
    else:
        o_ref[...] = acc.astype(o_ref.dtype)


def _mm(x, w, out_dtype, *, act=None, rope=None, seq=None):
    M, K = x.shape
    N = w.shape[1]
    tm = _pick(M, (1024, 512, 256, 128))
    tn = _pick(N, (1024, 512, 256, 128))
    in_specs = [pl.BlockSpec((tm, K), lambda i, j: (i, 0)),
                pl.BlockSpec((K, tn), lambda i, j: (0, j))]
    args = [x, w]
    if rope is not None:
        tm = _pick(seq, (tm, 512, 256, 128))
        in_specs[0] = pl.BlockSpec((tm, K), lambda i, j: (i, 0))
        nsb = seq // tm
        tab = pl.BlockSpec((tm, LANE), lambda i, j: (i % nsb, 0))
        in_specs += [tab, tab]
        args += list(rope)
    return pl.pallas_call(
        functools.partial(_mm_kernel, act=act, rope=rope is not None),
        grid=(M // tm, N // tn),
        in_specs=in_specs,
        out_specs=pl.BlockSpec((tm, tn), lambda i, j: (i, j)),
        out_shape=jax.ShapeDtypeStruct((M, N), out_dtype),
        compiler_params=_cparams(("parallel", "parallel")),
        name="proj_mm",
    )(*args)


LN_ROWS = 32
LN_COLS = 1024


def _mm_ln_kernel(*refs, nk, alpha, ple, emit_bf16):
    refs = list(refs)
    x_ref, w_ref, r_ref, g_ref, b_ref = refs[:5]
    rest = refs[5:]
    if ple:
        p_ref, wp_ref = rest[:2]
        rest = rest[2:]
    o_ref = rest[0]
    o16_ref = rest[1] if emit_bf16 else None
    k = pl.program_id(1)
    n_total = o_ref.shape[1]
    nc = _pick(n_total, (LN_COLS, LANE))
    x = x_ref[...]
    for c in range(n_total // nc):
        cols = slice(c * nc, (c + 1) * nc)
        part = _dot(x, w_ref[:, cols])

        @pl.when(k == 0)
        def _():
            o_ref[:, cols] = part

        @pl.when(k > 0)
        def _():
            o_ref[:, cols] += part

    @pl.when(k == nk - 1)
    def _():
        g = g_ref[...]
        b = b_ref[...]

        def body(c, carry):
            rows = pl.ds(pl.multiple_of(c * LN_ROWS, LN_ROWS), LN_ROWS)
            f = o_ref[rows, :]
            if ple:
                f = jax.nn.sigmoid(f) * _dot(p_ref[rows, :], wp_ref[...])
            y = alpha * r_ref[rows, :] + f
            mu = jnp.mean(y, axis=-1, keepdims=True)
            d = y - mu
            var = jnp.mean(d * d, axis=-1, keepdims=True)
            out = d * lax.rsqrt(var + LN_EPS) * g + b
            o_ref[rows, :] = out
            if emit_bf16:
                o16_ref[rows, :] = out.astype(BF16)
            return carry

        lax.fori_loop(0, o_ref.shape[0] // LN_ROWS, body, 0)


def _mm_ln(x, w, resid, g, b, alpha, *, ple=None, emit_bf16=True):
    M, K = x.shape
    N = w.shape[1]
    tm = _pick(M, (512, 256, 128))
    tk = _pick(K, (512, 256, 128))
    nk = K // tk
    in_specs = [pl.BlockSpec((tm, tk), lambda i, k: (i, k)),
                pl.BlockSpec((tk, N), lambda i, k: (k, 0)),
                pl.BlockSpec((tm, N), lambda i, k: (i, 0)),
                pl.BlockSpec((1, N), lambda i, k: (0, 0)),
                pl.BlockSpec((1, N), lambda i, k: (0, 0))]
    args = [x, w, resid, g.reshape(1, N), b.reshape(1, N)]
    if ple is not None:
        pp, wp = ple
        in_specs += [pl.BlockSpec((tm, pp.shape[1]), lambda i, k: (i, 0)),
                     pl.BlockSpec(wp.shape, lambda i, k: (0, 0))]
        args += [pp, wp]
    out_specs = [pl.BlockSpec((tm, N), lambda i, k: (i, 0))]
    out_shape = [jax.ShapeDtypeStruct((M, N), F32)]
    if emit_bf16:
        out_specs.append(pl.BlockSpec((tm, N), lambda i, k: (i, 0)))
        out_shape.append(jax.ShapeDtypeStruct((M, N), BF16))
    outs = pl.pallas_call(
        functools.partial(_mm_ln_kernel, nk=nk, alpha=alpha, ple=ple is not None, emit_bf16=emit_bf16),
        grid=(M // tm, nk),
        in_specs=in_specs,
        out_specs=out_specs,
        out_shape=out_shape,
        compiler_params=_cparams(("parallel", "arbitrary")),
        name="mm_res_ln",
    )(*args)
    return (outs[0], outs[1]) if emit_bf16 else (outs[0], None)


def _rope64(xr, cos, s_lo, s_hi):
    return xr * cos + pltpu.roll(xr, LANE - MLA_ROPE // 2, 1) * s_lo + pltpu.roll(xr, MLA_ROPE // 2, 1) * s_hi


def _mla_up_kernel(c_ref, gq_ref, gkv_ref, wq_ref, wk_ref, wv_ref, cos_ref, slo_ref, shi_ref,
                   q_ref, k_ref, v_ref, *, q_rank, kv_rank, heads):
    cos = cos_ref[...]
    s_lo = slo_ref[...]
    s_hi = shi_ref[...]

    def rms(t, g):
        return (t * lax.rsqrt(jnp.mean(t * t, axis=-1, keepdims=True) + RMS_EPS) * g).astype(BF16)

    nq = rms(c_ref[:, :q_rank], gq_ref[...])
    qf = _dot(nq, wq_ref[...])
    for h in range(heads):
        lo = h * MLA_QK_PAD
        q_ref[:, lo:lo + MLA_NOPE] = qf[:, lo:lo + MLA_NOPE].astype(BF16)
        q_ref[:, lo + MLA_NOPE:lo + MLA_QK_PAD] = _rope64(
            qf[:, lo + MLA_NOPE:lo + MLA_QK_PAD], cos, s_lo, s_hi).astype(BF16)
    nkv = rms(c_ref[:, q_rank:q_rank + kv_rank], gkv_ref[...])
    kf = _dot(nkv, wk_ref[...])
    v_ref[...] = _dot(nkv, wv_ref[...]).astype(BF16)
    kr = _rope64(c_ref[:, q_rank + kv_rank:q_rank + kv_rank + LANE], cos, s_lo, s_hi).astype(BF16)
    for h in range(heads):
        lo = h * MLA_QK_PAD
        k_ref[:, lo:lo + MLA_NOPE] = kf[:, h * MLA_NOPE:(h + 1) * MLA_NOPE].astype(BF16)
        k_ref[:, lo + MLA_NOPE:lo + MLA_QK_PAD] = kr


def _mla_up(cbuf, c_width, gq, gkv, wq, wk, wv, tabs, seq, heads):
    M = cbuf.shape[0]
    q_rank, kv_rank = gq.shape[0], gkv.shape[0]
    tm = _pick(seq, (512, 256, 128))
    nsb = seq // tm
    const = lambda shape: pl.BlockSpec(shape, lambda i: (0, 0))
    tab = pl.BlockSpec((tm, LANE), lambda i: (i % nsb, 0))
    row = lambda n: pl.BlockSpec((tm, n), lambda i: (i, 0))
    return pl.pallas_call(
        functools.partial(_mla_up_kernel, q_rank=q_rank, kv_rank=kv_rank, heads=heads),
        grid=(M // tm,),
        in_specs=[row(c_width), const((1, q_rank)), const((1, kv_rank)),
                  const(wq.shape), const(wk.shape), const(wv.shape), tab, tab, tab],
        out_specs=[row(heads * MLA_QK_PAD), row(heads * MLA_QK_PAD), row(heads * MLA_V)],
        out_shape=[jax.ShapeDtypeStruct((M, heads * MLA_QK_PAD), BF16),
                   jax.ShapeDtypeStruct((M, heads * MLA_QK_PAD), BF16),
                   jax.ShapeDtypeStruct((M, heads * MLA_V), BF16)],
        compiler_params=_cparams(("parallel",)),
        name="mla_up",
    )(cbuf, gq.reshape(1, -1), gkv.reshape(1, -1), wq, wk, wv, *tabs)


def _attn_specs(tq, seq, dq, dk, dv, q_off, k_off, v_off):
    return [pl.BlockSpec((1, tq, dq), lambda b, h, i: (b, i, q_off + h)),
            pl.BlockSpec((1, seq, dk), lambda b, h, i: (b, 0, k_off + h)),
            pl.BlockSpec((1, seq, dv), lambda b, h, i: (b, 0, v_off + h))]


def _sb_kernel(q_ref, k_ref, v_ref, tri_ref, o_ref, *, t, scale):
    i = pl.program_id(2)
    q = q_ref[0]
    tri = tri_ref[...]
    row = lax.broadcasted_iota(jnp.int32, (t, t), 0)
    col = lax.broadcasted_iota(jnp.int32, (t, t), 1)
    past = col < row

    def block(j, c, acc, masked):
        start = pl.multiple_of(j * t, t)
        ks = k_ref[0, pl.ds(start, t), :]
        vs = v_ref[0, pl.ds(start, t), :]
        z = _dot_nt(q, ks) * scale
        log_beta = jnp.minimum(z, 0.0) - jnp.log1p(jnp.exp(-jnp.abs(z)))
        log_1m = log_beta - z
        if masked:
            log_1m = jnp.where(past, log_1m, 0.0)
        hi = log_1m.astype(BF16)
        lo = (log_1m - hi.astype(F32)).astype(BF16)
        between = _dot(hi, tri) + _dot(lo, tri) + c
        w = jnp.exp(log_beta + between)
        if masked:
            w = jnp.where(past, w, 0.0)
        acc = acc + _dot(w.astype(BF16), vs)
        c = c + jnp.sum(log_1m, axis=-1, keepdims=True)
        return c, acc

    c0 = jnp.zeros((t, 1), F32)
    acc0 = jnp.zeros((t, v_ref.shape[2]), F32)
    c, acc = block(i, c0, acc0, True)

    def body(jj, carry):
        return block(i - 1 - jj, carry[0], carry[1], False)

    c, acc = lax.fori_loop(0, i, body, (c, acc))
    o_ref[0] = acc.astype(o_ref.dtype)


def _sb_attention(qkv, batch, seq, heads, q_off, k_off, v_off):
    t = _pick(seq, (256, 128))
    arr = qkv.reshape(batch, seq, -1)
    tri = (lax.broadcasted_iota(jnp.int32, (t, t), 0) > lax.broadcasted_iota(jnp.int32, (t, t), 1)).astype(BF16)
    specs = _attn_specs(t, seq, HEAD_DIM, HEAD_DIM, HEAD_DIM, q_off, k_off, v_off)
    specs.append(pl.BlockSpec((t, t), lambda b, h, i: (0, 0)))
    return pl.pallas_call(
        functools.partial(_sb_kernel, t=t, scale=HEAD_DIM ** -0.5),
        grid=(batch, heads, seq // t),
        in_specs=specs,
        out_specs=pl.BlockSpec((1, t, HEAD_DIM), lambda b, h, i: (b, i, h)),
        out_shape=jax.ShapeDtypeStruct((batch, seq, heads * HEAD_DIM), BF16),
        compiler_params=_cparams(("parallel", "parallel", "arbitrary")),
        name="sb_attn",
    )(arr, arr, arr, tri)


def _flash_step(q, ks, vs, m, l, acc, scale, mask):
    s = _dot_nt(q, ks) * scale
    if mask is not None:
        s = jnp.where(mask, s, -jnp.inf)
    m_new = jnp.maximum(m, jnp.max(s, axis=-1, keepdims=True))
    a = jnp.exp(m - m_new)
    p = jnp.exp(s - m_new)
    l = a * l + jnp.sum(p, axis=-1, keepdims=True)
    acc = a * acc + _dot(p.astype(BF16), vs)
    return m_new, l, acc


def _mla_kernel(q_ref, k_ref, v_ref, o_ref, *, t, scale):
    i = pl.program_id(2)
    q = q_ref[0]
    causal = lax.broadcasted_iota(jnp.int32, (t, t), 1) <= lax.broadcasted_iota(jnp.int32, (t, t), 0)

    def kv(j):
        start = pl.multiple_of(j * t, t)
        return k_ref[0, pl.ds(start, t), :], v_ref[0, pl.ds(start, t), :]

    def body(j, carry):
        ks, vs = kv(j)
        return _flash_step(q, ks, vs, *carry, scale, None)

    init = (jnp.full((t, 1), -jnp.inf, F32), jnp.zeros((t, 1), F32), jnp.zeros((t, v_ref.shape[2]), F32))
    carry = lax.fori_loop(0, i, body, init)
    ks, vs = kv(i)
    m, l, acc = _flash_step(q, ks, vs, *carry, scale, causal)
    o_ref[0] = (acc / l).astype(o_ref.dtype)


def _mla_attention(q, k, v, batch, seq, heads):
    t = _pick(seq, (256, 128))
    return pl.pallas_call(
        functools.partial(_mla_kernel, t=t, scale=(MLA_NOPE + MLA_ROPE) ** -0.5),
        grid=(batch, heads, seq // t),
        in_specs=_attn_specs(t, seq, MLA_QK_PAD, MLA_QK_PAD, MLA_V, 0, 0, 0),
        out_specs=pl.BlockSpec((1, t, MLA_V), lambda b, h, i: (b, i, h)),
        out_shape=jax.ShapeDtypeStruct((batch, seq, heads * MLA_V), BF16),
        compiler_params=_cparams(("parallel", "parallel", "arbitrary")),
        name="mla_attn",
    )(q.reshape(batch, seq, -1), k.reshape(batch, seq, -1), v.reshape(batch, seq, -1))


def _diff_kernel(q_ref, k_ref, v_ref, lam_ref, g_ref, o_ref, *, t, scale, lam_init):
    i = pl.program_id(2)
    d = HEAD_DIM
    q1 = q_ref[0, :, :d]
    q2 = q_ref[0, :, d:]
    causal = lax.broadcasted_iota(jnp.int32, (t, t), 1) <= lax.broadcasted_iota(jnp.int32, (t, t), 0)

    def step(j, carry, mask):
        start = pl.multiple_of(j * t, t)
        vs = v_ref[0, pl.ds(start, t), :]
        c1 = _flash_step(q1, k_ref[0, pl.ds(start, t), :d], vs, *carry[:3], scale, mask)
        c2 = _flash_step(q2, k_ref[0, pl.ds(start, t), d:], vs, *carry[3:], scale, mask)
        return c1 + c2

    one = (jnp.full((t, 1), -jnp.inf, F32), jnp.zeros((t, 1), F32), jnp.zeros((t, v_ref.shape[2]), F32))
    carry = lax.fori_loop(0, i, lambda j, c: step(j, c, None), one + one)
    m1, l1, a1, m2, l2, a2 = step(i, carry, causal)
    lp = lam_ref[...]
    lam = (jnp.exp(jnp.sum(lp[0:1] * lp[1:2], axis=-1, keepdims=True))
           - jnp.exp(jnp.sum(lp[2:3] * lp[3:4], axis=-1, keepdims=True)) + lam_init)
    o = a1 / l1 - lam * (a2 / l2)
    o = o * lax.rsqrt(jnp.mean(o * o, axis=-1, keepdims=True) + RMS_EPS) * g_ref[...]
    o_ref[0] = (o * (1.0 - lam_init)).astype(o_ref.dtype)


def _diff_attention(qk, q_off, k_off, vbuf, v_off, lam_params, subln_g, lam_init, batch, seq, heads):
    t = _pick(seq, (256, 128))
    dd = 2 * HEAD_DIM
    specs = [pl.BlockSpec((1, t, dd), lambda b, h, i: (b, i, q_off + h)),
             pl.BlockSpec((1, seq, dd), lambda b, h, i: (b, 0, k_off + h)),
             pl.BlockSpec((1, seq, dd), lambda b, h, i: (b, 0, v_off + h)),
             pl.BlockSpec(lam_params.shape, lambda b, h, i: (0, 0)),
             pl.BlockSpec((1, dd), lambda b, h, i: (0, 0))]
    return pl.pallas_call(
        functools.partial(_diff_kernel, t=t, scale=HEAD_DIM ** -0.5, lam_init=lam_init),
        grid=(batch, heads, seq // t),
        in_specs=specs,
        out_specs=pl.BlockSpec((1, t, dd), lambda b, h, i: (b, i, h)),
        out_shape=jax.ShapeDtypeStruct((batch, seq, heads * dd), BF16),
        compiler_params=_cparams(("parallel", "parallel", "arbitrary")),
        name="diff_attn",
    )(qk.reshape(batch, seq, -1), qk.reshape(batch, seq, -1), vbuf.reshape(batch, seq, -1),
      lam_params, subln_g.reshape(1, dd))


def _dil_kernel(q_ref, kp_ref, kc_ref, vp_ref, vc_ref, o_ref, acc_scr, m_scr, l_scr, *, scale):
    has_prev = pl.program_id(2) > 0
    ri = lax.broadcasted_iota(jnp.int32, (QBLOCK, QBLOCK), 0)
    ci = lax.broadcasted_iota(jnp.int32, (QBLOCK, QBLOCK), 1)
    cur_ok = ci <= ri
    prev_ok = ci >= ri
    prev_ok_first = ci >= ri + jnp.where(has_prev, 0, QBLOCK)

    def rows(n, r, dil):
        start = n * QBLOCK * dil + r
        return pl.ds(start, QBLOCK) if dil == 1 else pl.ds(start, QBLOCK, stride=dil)

    for bi, (window, dil) in enumerate(DIL_BRANCHES):
        assert window // dil == QBLOCK
        nblk = DIL_CHUNK // (QBLOCK * dil)
        for r in range(dil):
            for n in range(nblk):
                cur = rows(n, r, dil)
                q = q_ref[0, cur, :].astype(BF16)
                kc = kc_ref[0, cur, :].astype(BF16)
                vc = vc_ref[0, cur, :].astype(BF16)
                if n > 0:
                    prv = rows(n - 1, r, dil)
                    kp = kc_ref[0, prv, :].astype(BF16)
                    vp = vc_ref[0, prv, :].astype(BF16)
                    pmask = prev_ok
                else:
                    prv = rows(nblk - 1, r, dil)
                    kp = kp_ref[0, prv, :].astype(BF16)
                    vp = vp_ref[0, prv, :].astype(BF16)
                    pmask = prev_ok_first
                s_p = jnp.where(pmask, _dot_nt(q, kp) * scale, -jnp.inf)
                s_c = jnp.where(cur_ok, _dot_nt(q, kc) * scale, -jnp.inf)
                mx = jnp.maximum(jnp.max(s_p, axis=-1, keepdims=True), jnp.max(s_c, axis=-1, keepdims=True))
                mx = jnp.broadcast_to(mx, (QBLOCK, LANE))
                if bi == 0:
                    m_new = mx
                else:
                    m_old = m_scr[cur, :]
                    m_new = jnp.maximum(m_old, mx)
                e_p = jnp.exp(s_p - m_new)
                e_c = jnp.exp(s_c - m_new)
                lsum = jnp.sum(e_p, axis=-1, keepdims=True) + jnp.sum(e_c, axis=-1, keepdims=True)
                lsum = jnp.broadcast_to(lsum, (QBLOCK, LANE))
                pv = _dot(e_p.astype(BF16), vp) + _dot(e_c.astype(BF16), vc)
                if bi == 0:
                    acc_scr[cur, :] = pv
                    l_scr[cur, :] = lsum
                else:
                    a = jnp.exp(m_old - m_new)
                    acc_scr[cur, :] = a * acc_scr[cur, :] + pv
                    l_scr[cur, :] = a * l_scr[cur, :] + lsum
                m_scr[cur, :] = m_new
    o_ref[0] = (acc_scr[...] / l_scr[...]).astype(o_ref.dtype)


def _dil_attention(qk, q_off, k_off, vbuf, v_off, batch, seq, heads):
    assert seq % DIL_CHUNK == 0
    qk3 = qk.reshape(batch, seq, -1)
    v3 = vbuf.reshape(batch, seq, -1)
    blk = (1, DIL_CHUNK, HEAD_DIM)
    cur = lambda off: pl.BlockSpec(blk, lambda b, h, c: (b, c, off + h))
    prev = lambda off: pl.BlockSpec(blk, lambda b, h, c: (b, jnp.maximum(c - 1, 0), off + h))
    return pl.pallas_call(
        functools.partial(_dil_kernel, scale=HEAD_DIM ** -0.5),
        grid=(batch, heads, seq // DIL_CHUNK),
        in_specs=[cur(q_off), prev(k_off), cur(k_off), prev(v_off), cur(v_off)],
        out_specs=pl.BlockSpec(blk, lambda b, h, c: (b, c, h)),
        out_shape=jax.ShapeDtypeStruct((batch, seq, heads * HEAD_DIM), BF16),
        scratch_shapes=[pltpu.VMEM((DIL_CHUNK, LANE), F32)] * 3,
        compiler_params=_cparams(("parallel", "parallel", "arbitrary")),
        name="dil_attn",
    )(qk3, qk3, qk3, v3, v3)


def _rope_tables(seq):
    pos = jnp.arange(seq, dtype=jnp.int32).astype(F32)

    def cs(width):
        half = width // 2
        inv = ROPE_THETA ** (-jnp.arange(half, dtype=F32) / half)
        ang = pos[:, None] * inv[None, :]
        return jnp.cos(ang), jnp.sin(ang)

    cos, sin = cs(HEAD_DIM)
    full = (jnp.concatenate([cos, cos], axis=-1), jnp.concatenate([-sin, sin], axis=-1))
    cos, sin = cs(MLA_ROPE)
    z = jnp.zeros_like(cos)
    pad = jnp.zeros((seq, LANE - MLA_ROPE), F32)
    mla = (jnp.concatenate([cos, cos, pad], axis=-1),
           jnp.concatenate([-sin, z, pad], axis=-1),
           jnp.concatenate([z, sin, pad], axis=-1))
    return full, mla


def _layer(xf, xb, p_i, lam_init, alpha, tabs, dims, w_in, w_o, mla_q_norm, mla_w_uq, mla_kv_norm, mla_w_ukv,
           diff_lambda, diff_subln, ln_attn_g, ln_attn_b, w_ff1, w_ff2, ln_ff_g, ln_ff_b,
           w_ple_gate, w_ple_proj, ln_ple_g, ln_ple_b, last):
    batch, seq, d_model = dims
    rope_full, rope_mla = tabs
    mix_heads = d_model // HEAD_DIM
    sb_heads = dil_heads = mla_heads = mix_heads // 4
    diff_heads = mix_heads // 8
    q_rank, kv_rank = mla_q_norm.shape[0], mla_kv_norm.shape[0]
    hw = sb_heads * HEAD_DIM
    widths = (hw,) * 3 + (hw,) * 3 + (q_rank, kv_rank, MLA_ROPE) + (diff_heads * 2 * HEAD_DIM,) * 3
    offs = [0]
    for wd in widths:
        offs.append(offs[-1] + wd)
    col = lambda n: w_in[:, offs[n]:offs[n + 1]]
    a_q, a_k, a_v, b_q, b_k, b_v, c_q, c_kv, c_kr, d_q, d_k, d_v = (col(n) for n in range(12))

    w_plain = jnp.concatenate([a_q, a_k, a_v, d_v], axis=1).astype(BF16)
    w_rope_f = jnp.concatenate([b_q, b_k], axis=1).astype(BF16)
    w_rope_h = jnp.concatenate([d_q, d_k], axis=1).astype(BF16)
    c_used = q_rank + kv_rank + LANE
    c_width = -(-c_used // 1024) * 1024
    w_c = jnp.concatenate([c_q, c_kv, c_kr, jnp.zeros((d_model, c_width - c_used + LANE - MLA_ROPE), w_in.dtype),
                           b_v], axis=1).astype(BF16)

    h_plain = _mm(xb, w_plain, BF16)
    h_rope_f = _mm(xb, w_rope_f, F32, rope=rope_full, seq=seq)
    h_rope_h = _mm(xb, w_rope_h, BF16, rope=rope_full, seq=seq)
    h_c = _mm(xb, w_c, F32)

    y_a = _sb_attention(h_plain, batch, seq, sb_heads, 0, sb_heads, 2 * sb_heads)
    y_b = _dil_attention(h_rope_f, 0, dil_heads, h_c, c_width // HEAD_DIM, batch, seq, dil_heads)

    wq = mla_w_uq.reshape(q_rank, mla_heads, MLA_NOPE + MLA_ROPE)
    wq = jnp.pad(wq, ((0, 0), (0, 0), (0, MLA_QK_PAD - MLA_NOPE - MLA_ROPE))).reshape(q_rank, -1).astype(BF16)
    wkv = mla_w_ukv.reshape(kv_rank, mla_heads, MLA_NOPE + MLA_V)
    wk = wkv[:, :, :MLA_NOPE].reshape(kv_rank, -1).astype(BF16)
    wv = wkv[:, :, MLA_NOPE:].reshape(kv_rank, -1).astype(BF16)
    mq, mk, mv = _mla_up(h_c, c_width, mla_q_norm, mla_kv_norm, wq, wk, wv, rope_mla, seq, mla_heads)
    y_c = _mla_attention(mq, mk, mv, batch, seq, mla_heads)

    y_d = _diff_attention(h_rope_h, 0, diff_heads, h_plain, 3 * sb_heads // 2, diff_lambda, diff_subln,
                          lam_init, batch, seq, diff_heads)

    mix = jnp.concatenate([y_a, y_b, y_c, y_d], axis=-1).reshape(batch * seq, -1)
    xf, xb = _mm_ln(mix, w_o.astype(BF16), xf, ln_attn_g, ln_attn_b, alpha)

    u = _mm(xb, w_ff1.astype(BF16), BF16, act="relu2")
    xf, xb = _mm_ln(u, w_ff2.astype(BF16), xf, ln_ff_g, ln_ff_b, alpha)

    xf, xb = _mm_ln(xb, w_ple_gate.astype(BF16), xf, ln_ple_g, ln_ple_b, alpha,
                    ple=(p_i.astype(BF16), w_ple_proj.astype(BF16)), emit_bf16=not last)
    return xf, xb


def kernel(x, p, w_in, w_o, mla_q_norm, mla_w_uq, mla_kv_norm, mla_w_ukv, diff_lambda, diff_subln,
           ln_attn_g, ln_attn_b, w_ff1, w_ff2, ln_ff_g, ln_ff_b, w_ple_gate, w_ple_proj, ln_ple_g, ln_ple_b):
    batch, seq, d_model = x.shape
    depth = w_in.shape[0]
    alpha = (2 * depth) ** 0.25
    tabs = _rope_tables(seq)
    xf = x.reshape(batch * seq, d_model)
    xb = xf.astype(BF16)
    for i in range(depth):
        lam_init = 0.8 - 0.6 * math.exp(-0.3 * i)
        xf, xb = _layer(xf, xb, p[i].reshape(batch * seq, -1), lam_init, alpha, tabs, (batch, seq, d_model),
                        w_in[i], w_o[i], mla_q_norm[i], mla_w_uq[i], mla_kv_norm[i], mla_w_ukv[i],
                        diff_lambda[i], diff_subln[i], ln_attn_g[i], ln_attn_b[i], w_ff1[i], w_ff2[i],
                        ln_ff_g[i], ln_ff_b[i], w_ple_gate[i], w_ple_proj[i], ln_ple_g[i], ln_ple_b[i],
                        last=i == depth - 1)
    return xf.reshape(batch, seq, d_model)
```

```python
import functools
import math

import jax
import jax.numpy as jnp
from jax import lax
from jax.experimental import pallas as pl
from jax.experimental.pallas import tpu as pltpu

F32 = jnp.float32
BF16 = jnp.bfloat16

LANE = 128
HEAD_DIM = 128
QBLOCK = 128
ROPE_THETA = 10000.0
LN_EPS = 1e-5
RMS_EPS = 1e-6
MLA_NOPE = 128
MLA_ROPE = 64
MLA_V = 128
MLA_QK_PAD = 256
DIL_BRANCHES = ((128, 1), (512, 4), (2048, 16))
DIL_CHUNK = QBLOCK * 16
VMEM_LIMIT_BYTES = 56 * 1024 * 1024
SB_EXP_UNDERFLOW = -104.0
LOG2E = 1.4426950408889634
FLASH_TILE = 512


def _pick(n, cands):
    for c in cands:
        if n % c == 0:
            return c
    return n


def _cparams(sem):
    return pltpu.CompilerParams(dimension_semantics=sem, vmem_limit_bytes=VMEM_LIMIT_BYTES)


def _dot(a, b):
    return jnp.dot(a, b, preferred_element_type=F32)


def _dot_nt(a, b):
    return lax.dot_general(a, b, (((1,), (1,)), ((), ())), preferred_element_type=F32)


def _mm_kernel(*refs, act, rope):
    if rope:
        x_ref, w_ref, cos_ref, sin_ref, o_ref = refs
    else:
        x_ref, w_ref, o_ref = refs
    acc = _dot(x_ref[...], w_ref[...])
    if act == "relu2":
        acc = jnp.square(jnp.maximum(acc, 0.0))
    if rope:
        cos = cos_ref[...]
        sin = sin_ref[...]
        for c in range(acc.shape[1] // LANE):
            blk = acc[:, c * LANE:(c + 1) * LANE]
            rot = pltpu.roll(blk, LANE // 2, 1)
            o_ref[:, c * LANE:(c + 1) * LANE] = (blk * cos + rot * sin).astype(o_ref.dtype)
    else:
        o_ref[...] = acc.astype(o_ref.dtype)


def _mm(x, w, out_dtype, *, act=None, rope=None, seq=None):
    M, K = x.shape
    N = w.shape[1]
    tm = _pick(M, (1024, 512, 256, 128))
    tn = _pick(N, (1024, 512, 256, 128))
    in_specs = [pl.BlockSpec((tm, K), lambda i, j: (i, 0)),
                pl.BlockSpec((K, tn), lambda i, j: (0, j))]
    args = [x, w]
    if rope is not None:
        tm = _pick(seq, (tm, 512, 256, 128))
        in_specs[0] = pl.BlockSpec((tm, K), lambda i, j: (i, 0))
        nsb = seq // tm
        tab = pl.BlockSpec((tm, LANE), lambda i, j: (i % nsb, 0))
        in_specs += [tab, tab]
        args += list(rope)
    return pl.pallas_call(
        functools.partial(_mm_kernel, act=act, rope=rope is not None),
        grid=(M // tm, N // tn),
        in_specs=in_specs,
        out_specs=pl.BlockSpec((tm, tn), lambda i, j: (i, j)),
        out_shape=jax.ShapeDtypeStruct((M, N), out_dtype),
        compiler_params=_cparams(("parallel", "parallel")),
        name="proj_mm",
    )(*args)


LN_ROWS = 32
LN_COLS = 1024


def _mm_ln_kernel(*refs, nk, alpha, ple, emit_bf16):
    refs = list(refs)
    x_ref, w_ref, r_ref, g_ref, b_ref = refs[:5]
    rest = refs[5:]
    if ple:
        p_ref, wp_ref = rest[:2]
        rest = rest[2:]
    o_ref = rest[0]
    o16_ref = rest[1] if emit_bf16 else None
    k = pl.program_id(1)
    n_total = o_ref.shape[1]
    nc = _pick(n_total, (LN_COLS, LANE))

    @pl.when(k == 0)
    def _():
        o_ref[...] = jnp.zeros_like(o_ref)

    x = x_ref[...]
    for c in range(n_total // nc):
        cols = slice(c * nc, (c + 1) * nc)
        o_ref[:, cols] += _dot(x, w_ref[:, cols])

    @pl.when(k == nk - 1)
    def _():
        g = g_ref[...]
        b = b_ref[...]

        def body(c, carry):
            rows = pl.ds(pl.multiple_of(c * LN_ROWS, LN_ROWS), LN_ROWS)
            f = o_ref[rows, :]
            if ple:
                f = jax.nn.sigmoid(f) * _dot(p_ref[rows, :], wp_ref[...])
            y = alpha * r_ref[rows, :] + f
            mu = jnp.mean(y, axis=-1, keepdims=True)
            d = y - mu
            var = jnp.mean(d * d, axis=-1, keepdims=True)
            out = d * lax.rsqrt(var + LN_EPS) * g + b
            o_ref[rows, :] = out
            if emit_bf16:
                o16_ref[rows, :] = out.astype(BF16)
            return carry

        lax.fori_loop(0, o_ref.shape[0] // LN_ROWS, body, 0)


def _mm_ln(x, w, resid, g, b, alpha, *, ple=None, emit_bf16=True):
    M, K = x.shape
    N = w.shape[1]
    tm = _pick(M, (512, 256, 128))
    tk = _pick(K, (512, 256, 128))
    nk = K // tk
    in_specs = [pl.BlockSpec((tm, tk), lambda i, k: (i, k)),
                pl.BlockSpec((tk, N), lambda i, k: (k, 0)),
                pl.BlockSpec((tm, N), lambda i, k: (i, 0)),
                pl.BlockSpec((1, N), lambda i, k: (0, 0)),
                pl.BlockSpec((1, N), lambda i, k: (0, 0))]
    args = [x, w, resid, g.reshape(1, N), b.reshape(1, N)]
    if ple is not None:
        pp, wp = ple
        in_specs += [pl.BlockSpec((tm, pp.shape[1]), lambda i, k: (i, 0)),
                     pl.BlockSpec(wp.shape, lambda i, k: (0, 0))]
        args += [pp, wp]
    out_specs = [pl.BlockSpec((tm, N), lambda i, k: (i, 0))]
    out_shape = [jax.ShapeDtypeStruct((M, N), F32)]
    if emit_bf16:
        out_specs.append(pl.BlockSpec((tm, N), lambda i, k: (i, 0)))
        out_shape.append(jax.ShapeDtypeStruct((M, N), BF16))
    outs = pl.pallas_call(
        functools.partial(_mm_ln_kernel, nk=nk, alpha=alpha, ple=ple is not None, emit_bf16=emit_bf16),
        grid=(M // tm, nk),
        in_specs=in_specs,
        out_specs=out_specs,
        out_shape=out_shape,
        compiler_params=_cparams(("parallel", "arbitrary")),
        name="mm_res_ln",
    )(*args)
    return (outs[0], outs[1]) if emit_bf16 else (outs[0], None)


def _rope64(xr, cos, s_lo, s_hi):
    return xr * cos + pltpu.roll(xr, LANE - MLA_ROPE // 2, 1) * s_lo + pltpu.roll(xr, MLA_ROPE // 2, 1) * s_hi


def _mla_up_kernel(c_ref, gq_ref, gkv_ref, wq_ref, wk_ref, wv_ref, cos_ref, slo_ref, shi_ref,
                   q_ref, k_ref, v_ref, *, q_rank, kv_rank, heads):
    cos = cos_ref[...]
    s_lo = slo_ref[...]
    s_hi = shi_ref[...]

    def rms(t, g):
        return (t * lax.rsqrt(jnp.mean(t * t, axis=-1, keepdims=True) + RMS_EPS) * g).astype(BF16)

    nq = rms(c_ref[:, :q_rank], gq_ref[...])
    qf = _dot(nq, wq_ref[...])
    for h in range(heads):
        lo = h * MLA_QK_PAD
        q_ref[:, lo:lo + MLA_NOPE] = qf[:, lo:lo + MLA_NOPE].astype(BF16)
        q_ref[:, lo + MLA_NOPE:lo + MLA_QK_PAD] = _rope64(
            qf[:, lo + MLA_NOPE:lo + MLA_QK_PAD], cos, s_lo, s_hi).astype(BF16)
    nkv = rms(c_ref[:, q_rank:q_rank + kv_rank], gkv_ref[...])
    kf = _dot(nkv, wk_ref[...])
    v_ref[...] = _dot(nkv, wv_ref[...]).astype(BF16)
    kr = _rope64(c_ref[:, q_rank + kv_rank:q_rank + kv_rank + LANE], cos, s_lo, s_hi).astype(BF16)
    for h in range(heads):
        lo = h * MLA_QK_PAD
        k_ref[:, lo:lo + MLA_NOPE] = kf[:, h * MLA_NOPE:(h + 1) * MLA_NOPE].astype(BF16)
        k_ref[:, lo + MLA_NOPE:lo + MLA_QK_PAD] = kr


def _mla_up(cbuf, c_width, gq, gkv, wq, wk, wv, tabs, seq, heads):
    M = cbuf.shape[0]
    q_rank, kv_rank = gq.shape[0], gkv.shape[0]
    tm = _pick(seq, (512, 256, 128))
    nsb = seq // tm
    const = lambda shape: pl.BlockSpec(shape, lambda i: (0, 0))
    tab = pl.BlockSpec((tm, LANE), lambda i: (i % nsb, 0))
    row = lambda n: pl.BlockSpec((tm, n), lambda i: (i, 0))
    return pl.pallas_call(
        functools.partial(_mla_up_kernel, q_rank=q_rank, kv_rank=kv_rank, heads=heads),
        grid=(M // tm,),
        in_specs=[row(c_width), const((1, q_rank)), const((1, kv_rank)),
                  const(wq.shape), const(wk.shape), const(wv.shape), tab, tab, tab],
        out_specs=[row(heads * MLA_QK_PAD), row(heads * MLA_QK_PAD), row(heads * MLA_V)],
        out_shape=[jax.ShapeDtypeStruct((M, heads * MLA_QK_PAD), BF16),
                   jax.ShapeDtypeStruct((M, heads * MLA_QK_PAD), BF16),
                   jax.ShapeDtypeStruct((M, heads * MLA_V), BF16)],
        compiler_params=_cparams(("parallel",)),
        name="mla_up",
    )(cbuf, gq.reshape(1, -1), gkv.reshape(1, -1), wq, wk, wv, *tabs)


def _attn_specs(tq, seq, dq, dk, dv, q_off, k_off, v_off):
    return [pl.BlockSpec((1, tq, dq), lambda b, h, i: (b, i, q_off + h)),
            pl.BlockSpec((1, seq, dk), lambda b, h, i: (b, 0, k_off + h)),
            pl.BlockSpec((1, seq, dv), lambda b, h, i: (b, 0, v_off + h))]


def _sb_kernel(q_ref, k_ref, v_ref, tri_ref, o_ref, *, t, scale):
    i = pl.program_id(2)
    q = q_ref[0]
    tri = tri_ref[...]
    row = lax.broadcasted_iota(jnp.int32, (t, t), 0)
    col = lax.broadcasted_iota(jnp.int32, (t, t), 1)
    past = col < row

    def block(j, c, acc, masked):
        start = pl.multiple_of(j * t, t)
        ks = k_ref[0, pl.ds(start, t), :]
        vs = v_ref[0, pl.ds(start, t), :]
        z = _dot_nt(q, ks) * scale
        log_beta = jnp.minimum(z, 0.0) - jnp.log1p(jnp.exp(-jnp.abs(z)))
        log_1m = log_beta - z
        if masked:
            log_1m = jnp.where(past, log_1m, 0.0)
        hi = log_1m.astype(BF16)
        lo = (log_1m - hi.astype(F32)).astype(BF16)
        between = _dot(hi, tri) + _dot(lo, tri) + c
        w = jnp.exp(log_beta + between)
        if masked:
            w = jnp.where(past, w, 0.0)
        acc = acc + _dot(w.astype(BF16), vs)
        c = c + jnp.sum(log_1m, axis=-1, keepdims=True)
        return c, acc

    c0 = jnp.zeros((t, 1), F32)
    acc0 = jnp.zeros((t, v_ref.shape[2]), F32)
    c, acc = block(i, c0, acc0, True)

    def cond(carry):
        jj, c_max, _, _ = carry
        return jnp.logical_and(jj < i, c_max > SB_EXP_UNDERFLOW)

    def body(carry):
        jj, _, c, acc = carry
        c, acc = block(i - 1 - jj, c, acc, False)
        return jj + 1, jnp.max(c), c, acc

    _, _, c, acc = lax.while_loop(cond, body, (jnp.int32(0), jnp.max(c), c, acc))
    o_ref[0] = acc.astype(o_ref.dtype)


def _sb_attention(qkv, batch, seq, heads, q_off, k_off, v_off):
    t = _pick(seq, (256, 128))
    arr = qkv.reshape(batch, seq, -1)
    tri = (lax.broadcasted_iota(jnp.int32, (t, t), 0) > lax.broadcasted_iota(jnp.int32, (t, t), 1)).astype(BF16)
    specs = _attn_specs(t, seq, HEAD_DIM, HEAD_DIM, HEAD_DIM, q_off, k_off, v_off)
    specs.append(pl.BlockSpec((t, t), lambda b, h, i: (0, 0)))
    return pl.pallas_call(
        functools.partial(_sb_kernel, t=t, scale=HEAD_DIM ** -0.5),
        grid=(batch, heads, seq // t),
        in_specs=specs,
        out_specs=pl.BlockSpec((1, t, HEAD_DIM), lambda b, h, i: (b, i, h)),
        out_shape=jax.ShapeDtypeStruct((batch, seq, heads * HEAD_DIM), BF16),
        compiler_params=_cparams(("parallel", "parallel", "arbitrary")),
        name="sb_attn",
    )(arr, arr, arr, tri)


def _flash_step(q, ks, vs, m, l, acc, scale2, mask):
    s = _dot_nt(q, ks) * scale2
    if mask is not None:
        s = jnp.where(mask, s, -jnp.inf)
    m_new = jnp.maximum(m, jnp.max(s, axis=-1, keepdims=True))
    a = jnp.exp2(m - m_new)
    p = jnp.exp2(s - m_new)
    l = a * l + jnp.sum(p, axis=-1, keepdims=True)
    acc = a * acc + _dot(p.astype(BF16), vs)
    return m_new, l, acc


def _mla_kernel(q_ref, k_ref, v_ref, o_ref, *, t, scale):
    i = pl.program_id(2)
    q = q_ref[0]
    causal = lax.broadcasted_iota(jnp.int32, (t, t), 1) <= lax.broadcasted_iota(jnp.int32, (t, t), 0)

    def kv(j):
        start = pl.multiple_of(j * t, t)
        return k_ref[0, pl.ds(start, t), :], v_ref[0, pl.ds(start, t), :]

    def body(j, carry):
        ks, vs = kv(j)
        return _flash_step(q, ks, vs, *carry, scale, None)

    init = (jnp.full((t, 1), -jnp.inf, F32), jnp.zeros((t, 1), F32), jnp.zeros((t, v_ref.shape[2]), F32))
    carry = lax.fori_loop(0, i, body, init)
    ks, vs = kv(i)
    m, l, acc = _flash_step(q, ks, vs, *carry, scale, causal)
    o_ref[0] = (acc / l).astype(o_ref.dtype)


def _mla_attention(q, k, v, batch, seq, heads):
    t = _pick(seq, (FLASH_TILE, 256, 128))
    return pl.pallas_call(
        functools.partial(_mla_kernel, t=t, scale=(MLA_NOPE + MLA_ROPE) ** -0.5 * LOG2E),
        grid=(batch, heads, seq // t),
        in_specs=_attn_specs(t, seq, MLA_QK_PAD, MLA_QK_PAD, MLA_V, 0, 0, 0),
        out_specs=pl.BlockSpec((1, t, MLA_V), lambda b, h, i: (b, i, h)),
        out_shape=jax.ShapeDtypeStruct((batch, seq, heads * MLA_V), BF16),
        compiler_params=_cparams(("parallel", "parallel", "arbitrary")),
        name="mla_attn",
    )(q.reshape(batch, seq, -1), k.reshape(batch, seq, -1), v.reshape(batch, seq, -1))


def _diff_kernel(q_ref, k_ref, v_ref, lam_ref, g_ref, o_ref, *, t, scale, lam_init):
    i = pl.program_id(2)
    d = HEAD_DIM
    q1 = q_ref[0, :, :d]
    q2 = q_ref[0, :, d:]
    causal = lax.broadcasted_iota(jnp.int32, (t, t), 1) <= lax.broadcasted_iota(jnp.int32, (t, t), 0)

    def step(j, carry, mask):
        start = pl.multiple_of(j * t, t)
        vs = v_ref[0, pl.ds(start, t), :]
        c1 = _flash_step(q1, k_ref[0, pl.ds(start, t), :d], vs, *carry[:3], scale, mask)
        c2 = _flash_step(q2, k_ref[0, pl.ds(start, t), d:], vs, *carry[3:], scale, mask)
        return c1 + c2

    one = (jnp.full((t, 1), -jnp.inf, F32), jnp.zeros((t, 1), F32), jnp.zeros((t, v_ref.shape[2]), F32))
    carry = lax.fori_loop(0, i, lambda j, c: step(j, c, None), one + one)
    m1, l1, a1, m2, l2, a2 = step(i, carry, causal)
    lp = lam_ref[...]
    lam = (jnp.exp(jnp.sum(lp[0:1] * lp[1:2], axis=-1, keepdims=True))
           - jnp.exp(jnp.sum(lp[2:3] * lp[3:4], axis=-1, keepdims=True)) + lam_init)
    o = a1 / l1 - lam * (a2 / l2)
    o = o * lax.rsqrt(jnp.mean(o * o, axis=-1, keepdims=True) + RMS_EPS) * g_ref[...]
    o_ref[0] = (o * (1.0 - lam_init)).astype(o_ref.dtype)


def _diff_attention(qk, q_off, k_off, vbuf, v_off, lam_params, subln_g, lam_init, batch, seq, heads):
    t = _pick(seq, (FLASH_TILE, 256, 128))
    dd = 2 * HEAD_DIM
    specs = [pl.BlockSpec((1, t, dd), lambda b, h, i: (b, i, q_off + h)),
             pl.BlockSpec((1, seq, dd), lambda b, h, i: (b, 0, k_off + h)),
             pl.BlockSpec((1, seq, dd), lambda b, h, i: (b, 0, v_off + h)),
             pl.BlockSpec(lam_params.shape, lambda b, h, i: (0, 0)),
             pl.BlockSpec((1, dd), lambda b, h, i: (0, 0))]
    return pl.pallas_call(
        functools.partial(_diff_kernel, t=t, scale=HEAD_DIM ** -0.5 * LOG2E, lam_init=lam_init),
        grid=(batch, heads, seq // t),
        in_specs=specs,
        out_specs=pl.BlockSpec((1, t, dd), lambda b, h, i: (b, i, h)),
        out_shape=jax.ShapeDtypeStruct((batch, seq, heads * dd), BF16),
        compiler_params=_cparams(("parallel", "parallel", "arbitrary")),
        name="diff_attn",
    )(qk.reshape(batch, seq, -1), qk.reshape(batch, seq, -1), vbuf.reshape(batch, seq, -1),
      lam_params, subln_g.reshape(1, dd))


def _dil_kernel(q_ref, kp_ref, kc_ref, vp_ref, vc_ref, o_ref, acc_scr, m_scr, l_scr, *, scale):
    has_prev = pl.program_id(2) > 0
    ri = lax.broadcasted_iota(jnp.int32, (QBLOCK, QBLOCK), 0)
    ci = lax.broadcasted_iota(jnp.int32, (QBLOCK, QBLOCK), 1)
    cur_ok = ci <= ri
    prev_ok = ci >= ri
    prev_ok_first = ci >= ri + jnp.where(has_prev, 0, QBLOCK)

    def rows(n, r, dil):
        start = n * QBLOCK * dil + r
        return pl.ds(start, QBLOCK) if dil == 1 else pl.ds(start, QBLOCK, stride=dil)

    for bi, (window, dil) in enumerate(DIL_BRANCHES):
        assert window // dil == QBLOCK
        nblk = DIL_CHUNK // (QBLOCK * dil)
        for r in range(dil):
            for n in range(nblk):
                cur = rows(n, r, dil)
                q = q_ref[0, cur, :].astype(BF16)
                kc = kc_ref[0, cur, :].astype(BF16)
                vc = vc_ref[0, cur, :].astype(BF16)
                if n > 0:
                    prv = rows(n - 1, r, dil)
                    kp = kc_ref[0, prv, :].astype(BF16)
                    vp = vc_ref[0, prv, :].astype(BF16)
                    pmask = prev_ok
                else:
                    prv = rows(nblk - 1, r, dil)
                    kp = kp_ref[0, prv, :].astype(BF16)
                    vp = vp_ref[0, prv, :].astype(BF16)
                    pmask = prev_ok_first
                s_p = jnp.where(pmask, _dot_nt(q, kp) * scale, -jnp.inf)
                s_c = jnp.where(cur_ok, _dot_nt(q, kc) * scale, -jnp.inf)
                mx = jnp.maximum(jnp.max(s_p, axis=-1, keepdims=True), jnp.max(s_c, axis=-1, keepdims=True))
                mx = jnp.broadcast_to(mx, (QBLOCK, LANE))
                if bi == 0:
                    m_new = mx
                else:
                    m_old = m_scr[cur, :]
                    m_new = jnp.maximum(m_old, mx)
                e_p = jnp.exp(s_p - m_new)
                e_c = jnp.exp(s_c - m_new)
                lsum = jnp.sum(e_p, axis=-1, keepdims=True) + jnp.sum(e_c, axis=-1, keepdims=True)
                lsum = jnp.broadcast_to(lsum, (QBLOCK, LANE))
                pv = _dot(e_p.astype(BF16), vp) + _dot(e_c.astype(BF16), vc)
                if bi == 0:
                    acc_scr[cur, :] = pv
                    l_scr[cur, :] = lsum
                else:
                    a = jnp.exp(m_old - m_new)
                    acc_scr[cur, :] = a * acc_scr[cur, :] + pv
                    l_scr[cur, :] = a * l_scr[cur, :] + lsum
                m_scr[cur, :] = m_new
    o_ref[0] = (acc_scr[...] / l_scr[...]).astype(o_ref.dtype)


def _dil_attention(qk, q_off, k_off, vbuf, v_off, batch, seq, heads):
    assert seq % DIL_CHUNK == 0
    qk3 = qk.reshape(batch, seq, -1)
    v3 = vbuf.reshape(batch, seq, -1)
    blk = (1, DIL_CHUNK, HEAD_DIM)
    cur = lambda off: pl.BlockSpec(blk, lambda b, h, c: (b, c, off + h))
    prev = lambda off: pl.BlockSpec(blk, lambda b, h, c: (b, jnp.maximum(c - 1, 0), off + h))
    return pl.pallas_call(
        functools.partial(_dil_kernel, scale=HEAD_DIM ** -0.5),
        grid=(batch, heads, seq // DIL_CHUNK),
        in_specs=[cur(q_off), prev(k_off), cur(k_off), prev(v_off), cur(v_off)],
        out_specs=pl.BlockSpec(blk, lambda b, h, c: (b, c, h)),
        out_shape=jax.ShapeDtypeStruct((batch, seq, heads * HEAD_DIM), BF16),
        scratch_shapes=[pltpu.VMEM((DIL_CHUNK, LANE), F32)] * 3,
        compiler_params=_cparams(("parallel", "parallel", "arbitrary")),
        name="dil_attn",
    )(qk3, qk3, qk3, v3, v3)


def _rope_tables(seq):
    pos = jnp.arange(seq, dtype=jnp.int32).astype(F32)

    def cs(width):
        half = width // 2
        inv = ROPE_THETA ** (-jnp.arange(half, dtype=F32) / half)
        ang = pos[:, None] * inv[None, :]
        return jnp.cos(ang), jnp.sin(ang)

    cos, sin = cs(HEAD_DIM)
    full = (jnp.concatenate([cos, cos], axis=-1), jnp.concatenate([-sin, sin], axis=-1))
    cos, sin = cs(MLA_ROPE)
    z = jnp.zeros_like(cos)
    pad = jnp.zeros((seq, LANE - MLA_ROPE), F32)
    mla = (jnp.concatenate([cos, cos, pad], axis=-1),
           jnp.concatenate([-sin, z, pad], axis=-1),
           jnp.concatenate([z, sin, pad], axis=-1))
    return full, mla


def _layer(xf, xb, p_i, lam_init, alpha, tabs, dims, w_in, w_o, mla_q_norm, mla_w_uq, mla_kv_norm, mla_w_ukv,
           diff_lambda, diff_subln, ln_attn_g, ln_attn_b, w_ff1, w_ff2, ln_ff_g, ln_ff_b,
           w_ple_gate, w_ple_proj, ln_ple_g, ln_ple_b, last):
    batch, seq, d_model = dims
    rope_full, rope_mla = tabs
    mix_heads = d_model // HEAD_DIM
    sb_heads = dil_heads = mla_heads = mix_heads // 4
    diff_heads = mix_heads // 8
    q_rank, kv_rank = mla_q_norm.shape[0], mla_kv_norm.shape[0]
    hw = sb_heads * HEAD_DIM
    widths = (hw,) * 3 + (hw,) * 3 + (q_rank, kv_rank, MLA_ROPE) + (diff_heads * 2 * HEAD_DIM,) * 3
    offs = [0]
    for wd in widths:
        offs.append(offs[-1] + wd)
    col = lambda n: w_in[:, offs[n]:offs[n + 1]]
    a_q, a_k, a_v, b_q, b_k, b_v, c_q, c_kv, c_kr, d_q, d_k, d_v = (col(n) for n in range(12))

    w_plain = jnp.concatenate([a_q, a_k, a_v, d_v], axis=1).astype(BF16)
    w_rope_f = jnp.concatenate([b_q, b_k], axis=1).astype(BF16)
    w_rope_h = jnp.concatenate([d_q, d_k], axis=1).astype(BF16)
    c_used = q_rank + kv_rank + LANE
    c_width = -(-c_used // 1024) * 1024
    w_c = jnp.concatenate([c_q, c_kv, c_kr, jnp.zeros((d_model, c_width - c_used + LANE - MLA_ROPE), w_in.dtype),
                           b_v], axis=1).astype(BF16)

    h_plain = _mm(xb, w_plain, BF16)
    h_rope_f = _mm(xb, w_rope_f, F32, rope=rope_full, seq=seq)
    h_rope_h = _mm(xb, w_rope_h, BF16, rope=rope_full, seq=seq)
    h_c = _mm(xb, w_c, F32)

    y_a = _sb_attention(h_plain, batch, seq, sb_heads, 0, sb_heads, 2 * sb_heads)
    y_b = _dil_attention(h_rope_f, 0, dil_heads, h_c, c_width // HEAD_DIM, batch, seq, dil_heads)

    wq = mla_w_uq.reshape(q_rank, mla_heads, MLA_NOPE + MLA_ROPE)
    wq = jnp.pad(wq, ((0, 0), (0, 0), (0, MLA_QK_PAD - MLA_NOPE - MLA_ROPE))).reshape(q_rank, -1).astype(BF16)
    wkv = mla_w_ukv.reshape(kv_rank, mla_heads, MLA_NOPE + MLA_V)
    wk = wkv[:, :, :MLA_NOPE].reshape(kv_rank, -1).astype(BF16)
    wv = wkv[:, :, MLA_NOPE:].reshape(kv_rank, -1).astype(BF16)
    mq, mk, mv = _mla_up(h_c, c_width, mla_q_norm, mla_kv_norm, wq, wk, wv, rope_mla, seq, mla_heads)
    y_c = _mla_attention(mq, mk, mv, batch, seq, mla_heads)

    y_d = _diff_attention(h_rope_h, 0, diff_heads, h_plain, 3 * sb_heads // 2, diff_lambda, diff_subln,
                          lam_init, batch, seq, diff_heads)

    mix = jnp.concatenate([y_a, y_b, y_c, y_d], axis=-1).reshape(batch * seq, -1)
    xf, xb = _mm_ln(mix, w_o.astype(BF16), xf, ln_attn_g, ln_attn_b, alpha)

    u = _mm(xb, w_ff1.astype(BF16), BF16, act="relu2")
    xf, xb = _mm_ln(u, w_ff2.astype(BF16), xf, ln_ff_g, ln_ff_b, alpha)

    xf, xb = _mm_ln(xb, w_ple_gate.astype(BF16), xf, ln_ple_g, ln_ple_b, alpha,
                    ple=(p_i.astype(BF16), w_ple_proj.astype(BF16)), emit_bf16=not last)
    return xf, xb


def kernel(x, p, w_in, w_o, mla_q_norm, mla_w_uq, mla_kv_norm, mla_w_ukv, diff_lambda, diff_subln,
           ln_attn_g, ln_attn_b, w_ff1, w_ff2, ln_ff_g, ln_ff_b, w_ple_gate, w_ple_proj, ln_ple_g, ln_ple_b):
    batch, seq, d_model = x.shape
    depth = w_in.shape[0]
    alpha = (2 * depth) ** 0.25
    tabs = _rope_tables(seq)
    xf = x.reshape(batch * seq, d_model)
    xb = xf.astype(BF16)
    for i in range(depth):
        lam_init = 0.8 - 0.6 * math.exp(-0.3 * i)
        xf, xb = _layer(xf, xb, p[i].reshape(batch * seq, -1), lam_init, alpha, tabs, (batch, seq, d_model),
                        w_in[i], w_o[i], mla_q_norm[i], mla_w_uq[i], mla_kv_norm[i], mla_w_ukv[i],
                        diff_lambda[i], diff_subln[i], ln_attn_g[i], ln_attn_b[i], w_ff1[i], w_ff2[i],
                        ln_ff_g[i], ln_ff_b[i], w_ple_gate[i], w_ple_proj[i], ln_ple_g[i], ln_ple_b[i],
                        last=i == depth - 1)
    return xf.reshape(batch, seq, d_model)
```

```python
import functools
import math

import jax
import jax.numpy as jnp
from jax import lax
from jax.experimental import pallas as pl
from jax.experimental.pallas import tpu as pltpu

F32 = jnp.float32
BF16 = jnp.bfloat16

LANE = 128
HEAD_DIM = 128
QBLOCK = 128
ROPE_THETA = 10000.0
LN_EPS = 1e-5
RMS_EPS = 1e-6
MLA_NOPE = 128
MLA_ROPE = 64
MLA_V = 128
MLA_QK_PAD = 256
DIL_BRANCHES = ((128, 1), (512, 4), (2048, 16))
DIL_CHUNK = QBLOCK * 16
VMEM_LIMIT_BYTES = 56 * 1024 * 1024
SB_EXP_UNDERFLOW = -104.0
LOG2E = 1.4426950408889634
FLASH_TQ = 1024
FLASH_TK = 512
DIL_MERGE_ROWS = 256


def _pick(n, cands):
    for c in cands:
        if n % c == 0:
            return c
    return n


def _cparams(sem):
    return pltpu.CompilerParams(dimension_semantics=sem, vmem_limit_bytes=VMEM_LIMIT_BYTES)


def _dot(a, b):
    return jnp.dot(a, b, preferred_element_type=F32)


def _dot_nt(a, b):
    return lax.dot_general(a, b, (((1,), (1,)), ((), ())), preferred_element_type=F32)


def _mm_kernel(*refs, act, rope):
    if rope:
        x_ref, w_ref, cos_ref, sin_ref, o_ref = refs
    else:
        x_ref, w_ref, o_ref = refs
    acc = _dot(x_ref[...], w_ref[...])
    if act == "relu2":
        acc = jnp.square(jnp.maximum(acc, 0.0))
    if rope:
        cos = cos_ref[...]
        sin = sin_ref[...]
        for c in range(acc.shape[1] // LANE):
            blk = acc[:, c * LANE:(c + 1) * LANE]
            rot = pltpu.roll(blk, LANE // 2, 1)
            o_ref[:, c * LANE:(c + 1) * LANE] = (blk * cos + rot * sin).astype(o_ref.dtype)
    else:
        o_ref[...] = acc.astype(o_ref.dtype)


def _mm(x, w, out_dtype, *, act=None, rope=None, seq=None):
    M, K = x.shape
    N = w.shape[1]
    tm = _pick(M, (1024, 512, 256, 128))
    tn = _pick(N, (1024, 512, 256, 128))
    in_specs = [pl.BlockSpec((tm, K), lambda i, j: (i, 0)),
                pl.BlockSpec((K, tn), lambda i, j: (0, j))]
    args = [x, w]
    if rope is not None:
        tm = _pick(seq, (tm, 512, 256, 128))
        in_specs[0] = pl.BlockSpec((tm, K), lambda i, j: (i, 0))
        nsb = seq // tm
        tab = pl.BlockSpec((tm, LANE), lambda i, j: (i % nsb, 0))
        in_specs += [tab, tab]
        args += list(rope)
    return pl.pallas_call(
        functools.partial(_mm_kernel, act=act, rope=rope is not None),
        grid=(M // tm, N // tn),
        in_specs=in_specs,
        out_specs=pl.BlockSpec((tm, tn), lambda i, j: (i, j)),
        out_shape=jax.ShapeDtypeStruct((M, N), out_dtype),
        compiler_params=_cparams(("parallel", "parallel")),
        name="proj_mm",
    )(*args)


LN_ROWS = 32
LN_COLS = 1024


def _mm_ln_kernel(*refs, nk, alpha, ple, emit_bf16):
    refs = list(refs)
    x_ref, w_ref, r_ref, g_ref, b_ref = refs[:5]
    rest = refs[5:]
    if ple:
        p_ref, wp_ref = rest[:2]
        rest = rest[2:]
    o_ref = rest[0]
    o16_ref = rest[1] if emit_bf16 else None
    k = pl.program_id(1)
    n_total = o_ref.shape[1]
    nc = _pick(n_total, (LN_COLS, LANE))

    @pl.when(k == 0)
    def _():
        o_ref[...] = jnp.zeros_like(o_ref)

    x = x_ref[...]
    for c in range(n_total // nc):
        cols = slice(c * nc, (c + 1) * nc)
        o_ref[:, cols] += _dot(x, w_ref[:, cols])

    @pl.when(k == nk - 1)
    def _():
        g = g_ref[...]
        b = b_ref[...]

        def body(c, carry):
            rows = pl.ds(pl.multiple_of(c * LN_ROWS, LN_ROWS), LN_ROWS)
            f = o_ref[rows, :]
            if ple:
                f = jax.nn.sigmoid(f) * _dot(p_ref[rows, :], wp_ref[...])
            y = alpha * r_ref[rows, :] + f
            mu = jnp.mean(y, axis=-1, keepdims=True)
            d = y - mu
            var = jnp.mean(d * d, axis=-1, keepdims=True)
            out = d * lax.rsqrt(var + LN_EPS) * g + b
            o_ref[rows, :] = out
            if emit_bf16:
                o16_ref[rows, :] = out.astype(BF16)
            return carry

        lax.fori_loop(0, o_ref.shape[0] // LN_ROWS, body, 0)


def _mm_ln(x, w, resid, g, b, alpha, *, ple=None, emit_bf16=True):
    M, K = x.shape
    N = w.shape[1]
    tm = _pick(M, (512, 256, 128))
    tk = _pick(K, (512, 256, 128))
    nk = K // tk
    in_specs = [pl.BlockSpec((tm, tk), lambda i, k: (i, k)),
                pl.BlockSpec((tk, N), lambda i, k: (k, 0)),
                pl.BlockSpec((tm, N), lambda i, k: (i, 0)),
                pl.BlockSpec((1, N), lambda i, k: (0, 0)),
                pl.BlockSpec((1, N), lambda i, k: (0, 0))]
    args = [x, w, resid, g.reshape(1, N), b.reshape(1, N)]
    if ple is not None:
        pp, wp = ple
        in_specs += [pl.BlockSpec((tm, pp.shape[1]), lambda i, k: (i, 0)),
                     pl.BlockSpec(wp.shape, lambda i, k: (0, 0))]
        args += [pp, wp]
    out_specs = [pl.BlockSpec((tm, N), lambda i, k: (i, 0))]
    out_shape = [jax.ShapeDtypeStruct((M, N), F32)]
    if emit_bf16:
        out_specs.append(pl.BlockSpec((tm, N), lambda i, k: (i, 0)))
        out_shape.append(jax.ShapeDtypeStruct((M, N), BF16))
    outs = pl.pallas_call(
        functools.partial(_mm_ln_kernel, nk=nk, alpha=alpha, ple=ple is not None, emit_bf16=emit_bf16),
        grid=(M // tm, nk),
        in_specs=in_specs,
        out_specs=out_specs,
        out_shape=out_shape,
        compiler_params=_cparams(("parallel", "arbitrary")),
        name="mm_res_ln",
    )(*args)
    return (outs[0], outs[1]) if emit_bf16 else (outs[0], None)


def _rope64(xr, cos, s_lo, s_hi):
    return xr * cos + pltpu.roll(xr, LANE - MLA_ROPE // 2, 1) * s_lo + pltpu.roll(xr, MLA_ROPE // 2, 1) * s_hi


def _mla_up_kernel(c_ref, gq_ref, gkv_ref, wq_ref, wk_ref, wv_ref, cos_ref, slo_ref, shi_ref,
                   q_ref, k_ref, v_ref, *, q_rank, kv_rank, heads):
    cos = cos_ref[...]
    s_lo = slo_ref[...]
    s_hi = shi_ref[...]

    def rms(t, g):
        return (t * lax.rsqrt(jnp.mean(t * t, axis=-1, keepdims=True) + RMS_EPS) * g).astype(BF16)

    nq = rms(c_ref[:, :q_rank], gq_ref[...])
    qf = _dot(nq, wq_ref[...])
    for h in range(heads):
        lo = h * MLA_QK_PAD
        q_ref[:, lo:lo + MLA_NOPE] = qf[:, lo:lo + MLA_NOPE].astype(BF16)
        q_ref[:, lo + MLA_NOPE:lo + MLA_QK_PAD] = _rope64(
            qf[:, lo + MLA_NOPE:lo + MLA_QK_PAD], cos, s_lo, s_hi).astype(BF16)
    nkv = rms(c_ref[:, q_rank:q_rank + kv_rank], gkv_ref[...])
    kf = _dot(nkv, wk_ref[...])
    v_ref[...] = _dot(nkv, wv_ref[...]).astype(BF16)
    kr = _rope64(c_ref[:, q_rank + kv_rank:q_rank + kv_rank + LANE], cos, s_lo, s_hi).astype(BF16)
    for h in range(heads):
        lo = h * MLA_QK_PAD
        k_ref[:, lo:lo + MLA_NOPE] = kf[:, h * MLA_NOPE:(h + 1) * MLA_NOPE].astype(BF16)
        k_ref[:, lo + MLA_NOPE:lo + MLA_QK_PAD] = kr


def _mla_up(cbuf, c_width, gq, gkv, wq, wk, wv, tabs, seq, heads):
    M = cbuf.shape[0]
    q_rank, kv_rank = gq.shape[0], gkv.shape[0]
    tm = _pick(seq, (512, 256, 128))
    nsb = seq // tm
    const = lambda shape: pl.BlockSpec(shape, lambda i: (0, 0))
    tab = pl.BlockSpec((tm, LANE), lambda i: (i % nsb, 0))
    row = lambda n: pl.BlockSpec((tm, n), lambda i: (i, 0))
    return pl.pallas_call(
        functools.partial(_mla_up_kernel, q_rank=q_rank, kv_rank=kv_rank, heads=heads),
        grid=(M // tm,),
        in_specs=[row(c_width), const((1, q_rank)), const((1, kv_rank)),
                  const(wq.shape), const(wk.shape), const(wv.shape), tab, tab, tab],
        out_specs=[row(heads * MLA_QK_PAD), row(heads * MLA_QK_PAD), row(heads * MLA_V)],
        out_shape=[jax.ShapeDtypeStruct((M, heads * MLA_QK_PAD), BF16),
                   jax.ShapeDtypeStruct((M, heads * MLA_QK_PAD), BF16),
                   jax.ShapeDtypeStruct((M, heads * MLA_V), BF16)],
        compiler_params=_cparams(("parallel",)),
        name="mla_up",
    )(cbuf, gq.reshape(1, -1), gkv.reshape(1, -1), wq, wk, wv, *tabs)


def _attn_specs(tq, seq, dq, dk, dv, q_off, k_off, v_off):
    return [pl.BlockSpec((1, tq, dq), lambda b, h, i: (b, i, q_off + h)),
            pl.BlockSpec((1, seq, dk), lambda b, h, i: (b, 0, k_off + h)),
            pl.BlockSpec((1, seq, dv), lambda b, h, i: (b, 0, v_off + h))]


def _sb_kernel(q_ref, k_ref, v_ref, tri_ref, o_ref, *, t, scale):
    i = pl.program_id(2)
    q = q_ref[0]
    tri = tri_ref[...]
    row = lax.broadcasted_iota(jnp.int32, (t, t), 0)
    col = lax.broadcasted_iota(jnp.int32, (t, t), 1)
    past = col < row

    def block(j, c, acc, keep):
        start = pl.multiple_of(j * t, t)
        ks = k_ref[0, pl.ds(start, t), :]
        vs = v_ref[0, pl.ds(start, t), :]
        z = _dot_nt(q, ks) * scale
        log_beta = jnp.minimum(z, 0.0) - jnp.log1p(jnp.exp(-jnp.abs(z)))
        log_1m = log_beta - z
        if keep is not None:
            log_1m = jnp.where(keep, log_1m, 0.0)
        hi = log_1m.astype(BF16)
        lo = (log_1m - hi.astype(F32)).astype(BF16)
        between = _dot(hi, tri) + _dot(lo, tri) + c
        w = jnp.exp(log_beta + between)
        if keep is not None:
            w = jnp.where(keep, w, 0.0)
        acc = acc + _dot(w.astype(BF16), vs)
        c = c + jnp.sum(log_1m, axis=-1, keepdims=True)
        return c, acc

    c0 = jnp.zeros((t, 1), F32)
    acc0 = jnp.zeros((t, v_ref.shape[2]), F32)
    c, acc = block(i, c0, acc0, past)
    c, acc = block(jnp.maximum(i - 1, 0), c, acc, row < jnp.where(i > 0, t, 0))

    def cond(carry):
        jj, c_max, _, _ = carry
        return jnp.logical_and(jj < i, c_max > SB_EXP_UNDERFLOW)

    def body(carry):
        jj, _, c, acc = carry
        c, acc = block(i - 1 - jj, c, acc, None)
        return jj + 1, jnp.max(c), c, acc

    _, _, c, acc = lax.while_loop(cond, body, (jnp.int32(1), jnp.max(c), c, acc))
    o_ref[0] = acc.astype(o_ref.dtype)


def _sb_attention(qkv, batch, seq, heads, q_off, k_off, v_off):
    t = _pick(seq, (256, 128))
    arr = qkv.reshape(batch, seq, -1)
    tri = (lax.broadcasted_iota(jnp.int32, (t, t), 0) > lax.broadcasted_iota(jnp.int32, (t, t), 1)).astype(BF16)
    specs = _attn_specs(t, seq, HEAD_DIM, HEAD_DIM, HEAD_DIM, q_off, k_off, v_off)
    specs.append(pl.BlockSpec((t, t), lambda b, h, i: (0, 0)))
    return pl.pallas_call(
        functools.partial(_sb_kernel, t=t, scale=HEAD_DIM ** -0.5),
        grid=(batch, heads, seq // t),
        in_specs=specs,
        out_specs=pl.BlockSpec((1, t, HEAD_DIM), lambda b, h, i: (b, i, h)),
        out_shape=jax.ShapeDtypeStruct((batch, seq, heads * HEAD_DIM), BF16),
        compiler_params=_cparams(("parallel", "parallel", "arbitrary")),
        name="sb_attn",
    )(arr, arr, arr, tri)


def _softmax_pv(s, vs, m, l, acc, mask):
    if mask is not None:
        s = jnp.where(mask, s, -jnp.inf)
    m_new = jnp.maximum(m, jnp.max(s, axis=-1, keepdims=True))
    a = jnp.exp2(m - m_new)
    p = jnp.exp2(s - m_new)
    l = a * l + jnp.sum(p, axis=-1, keepdims=True)
    acc = a * acc + _dot(p.astype(BF16), vs)
    return m_new, l, acc


def _flash_init(t, dv):
    return jnp.full((t, 1), -jnp.inf, F32), jnp.zeros((t, 1), F32), jnp.zeros((t, dv), F32)


def _causal_sweep(i, tq, tk, step, init):
    nsub = tq // tk
    row = lax.broadcasted_iota(jnp.int32, (tq, tk), 0)
    col = lax.broadcasted_iota(jnp.int32, (tq, tk), 1)
    carry = lax.fori_loop(0, i * nsub, lambda j, c: step(j, c, None), init)
    for u in range(nsub):
        carry = step(i * nsub + u, carry, col + u * tk <= row)
    return carry


def _flash_tiles(seq):
    tq = _pick(seq, (FLASH_TQ, 512, 256, 128))
    return tq, _pick(tq, (FLASH_TK, 256, 128))


def _mla_kernel(q_ref, k_ref, v_ref, o_ref, *, tq, tk, scale):
    q = q_ref[0]

    def step(j, carry, mask):
        rows = pl.ds(pl.multiple_of(j * tk, tk), tk)
        s = _dot_nt(q, k_ref[0, rows, :]) * scale
        return _softmax_pv(s, v_ref[0, rows, :], *carry, mask)

    m, l, acc = _causal_sweep(pl.program_id(2), tq, tk, step, _flash_init(tq, v_ref.shape[2]))
    o_ref[0] = (acc / l).astype(o_ref.dtype)


def _mla_attention(q, k, v, batch, seq, heads):
    t, tk = _flash_tiles(seq)
    return pl.pallas_call(
        functools.partial(_mla_kernel, tq=t, tk=tk, scale=(MLA_NOPE + MLA_ROPE) ** -0.5 * LOG2E),
        grid=(batch, heads, seq // t),
        in_specs=_attn_specs(t, seq, MLA_QK_PAD, MLA_QK_PAD, MLA_V, 0, 0, 0),
        out_specs=pl.BlockSpec((1, t, MLA_V), lambda b, h, i: (b, i, h)),
        out_shape=jax.ShapeDtypeStruct((batch, seq, heads * MLA_V), BF16),
        compiler_params=_cparams(("parallel", "parallel", "arbitrary")),
        name="mla_attn",
    )(q.reshape(batch, seq, -1), k.reshape(batch, seq, -1), v.reshape(batch, seq, -1))


def _diff_kernel(q_ref, k_ref, v_ref, lam_ref, g_ref, o_ref, *, tq, tk, scale, lam_init):
    d = HEAD_DIM
    q1 = q_ref[0, :, :d]
    q2 = q_ref[0, :, d:]

    def step(j, carry, mask):
        rows = pl.ds(pl.multiple_of(j * tk, tk), tk)
        vs = v_ref[0, rows, :]
        s1 = _dot_nt(q1, k_ref[0, rows, :d]) * scale
        s2 = _dot_nt(q2, k_ref[0, rows, d:]) * scale
        return _softmax_pv(s1, vs, *carry[:3], mask) + _softmax_pv(s2, vs, *carry[3:], mask)

    one = _flash_init(tq, v_ref.shape[2])
    m1, l1, a1, m2, l2, a2 = _causal_sweep(pl.program_id(2), tq, tk, step, one + one)
    lp = lam_ref[...]
    lam = (jnp.exp(jnp.sum(lp[0:1] * lp[1:2], axis=-1, keepdims=True))
           - jnp.exp(jnp.sum(lp[2:3] * lp[3:4], axis=-1, keepdims=True)) + lam_init)
    o = a1 / l1 - lam * (a2 / l2)
    o = o * lax.rsqrt(jnp.mean(o * o, axis=-1, keepdims=True) + RMS_EPS) * g_ref[...]
    o_ref[0] = (o * (1.0 - lam_init)).astype(o_ref.dtype)


def _diff_attention(qk, q_off, k_off, vbuf, v_off, lam_params, subln_g, lam_init, batch, seq, heads):
    t, tk = _flash_tiles(seq)
    dd = 2 * HEAD_DIM
    specs = [pl.BlockSpec((1, t, dd), lambda b, h, i: (b, i, q_off + h)),
             pl.BlockSpec((1, seq, dd), lambda b, h, i: (b, 0, k_off + h)),
             pl.BlockSpec((1, seq, dd), lambda b, h, i: (b, 0, v_off + h)),
             pl.BlockSpec(lam_params.shape, lambda b, h, i: (0, 0)),
             pl.BlockSpec((1, dd), lambda b, h, i: (0, 0))]
    return pl.pallas_call(
        functools.partial(_diff_kernel, tq=t, tk=tk, scale=HEAD_DIM ** -0.5 * LOG2E, lam_init=lam_init),
        grid=(batch, heads, seq // t),
        in_specs=specs,
        out_specs=pl.BlockSpec((1, t, dd), lambda b, h, i: (b, i, h)),
        out_shape=jax.ShapeDtypeStruct((batch, seq, heads * dd), BF16),
        compiler_params=_cparams(("parallel", "parallel", "arbitrary")),
        name="diff_attn",
    )(qk.reshape(batch, seq, -1), qk.reshape(batch, seq, -1), vbuf.reshape(batch, seq, -1),
      lam_params, subln_g.reshape(1, dd))


def _dil_kernel(q_ref, kp_ref, kc_ref, vp_ref, vc_ref, o_ref, acc_scr, m_scr, l_scr, *, scale):
    has_prev = pl.program_id(2) > 0
    ri = lax.broadcasted_iota(jnp.int32, (QBLOCK, QBLOCK), 0)
    ci = lax.broadcasted_iota(jnp.int32, (QBLOCK, QBLOCK), 1)
    cur_ok = ci <= ri
    prev_ok = ci >= ri
    prev_ok_first = ci >= ri + jnp.where(has_prev, 0, QBLOCK)

    def rows(n, r, dil):
        start = n * QBLOCK * dil + r
        return pl.ds(start, QBLOCK) if dil == 1 else pl.ds(start, QBLOCK, stride=dil)

    for bi, (window, dil) in enumerate(DIL_BRANCHES):
        assert window // dil == QBLOCK
        nblk = DIL_CHUNK // (QBLOCK * dil)
        for r in range(dil):
            kv_prev = None
            for n in range(nblk):
                cur = rows(n, r, dil)
                q = q_ref[0, cur, :].astype(BF16)
                kc = kc_ref[0, cur, :].astype(BF16)
                vc = vc_ref[0, cur, :].astype(BF16)
                if n > 0:
                    kp, vp = kv_prev
                    pmask = prev_ok
                else:
                    prv = rows(nblk - 1, r, dil)
                    kp = kp_ref[0, prv, :].astype(BF16)
                    vp = vp_ref[0, prv, :].astype(BF16)
                    pmask = prev_ok_first
                kv_prev = (kc, vc)
                s_p = jnp.where(pmask, _dot_nt(q, kp) * scale, -jnp.inf)
                s_c = jnp.where(cur_ok, _dot_nt(q, kc) * scale, -jnp.inf)
                mx = jnp.maximum(jnp.max(s_p, axis=-1, keepdims=True), jnp.max(s_c, axis=-1, keepdims=True))
                mx = jnp.broadcast_to(mx, (QBLOCK, LANE))
                e_p = jnp.exp2(s_p - mx)
                e_c = jnp.exp2(s_c - mx)
                lsum = jnp.sum(e_p, axis=-1, keepdims=True) + jnp.sum(e_c, axis=-1, keepdims=True)
                acc_scr[bi, cur, :] = _dot(e_p.astype(BF16), vp) + _dot(e_c.astype(BF16), vc)
                m_scr[bi, cur, :] = mx
                l_scr[bi, cur, :] = jnp.broadcast_to(lsum, (QBLOCK, LANE))

    nb = len(DIL_BRANCHES)
    for c in range(DIL_CHUNK // DIL_MERGE_ROWS):
        rs = pl.ds(c * DIL_MERGE_ROWS, DIL_MERGE_ROWS)
        ms = [m_scr[b, rs, :] for b in range(nb)]
        top = functools.reduce(jnp.maximum, ms)
        ws = [jnp.exp2(m - top) for m in ms]
        num = sum(w * acc_scr[b, rs, :] for b, w in enumerate(ws))
        den = sum(w * l_scr[b, rs, :] for b, w in enumerate(ws))
        o_ref[0, rs, :] = (num / den).astype(o_ref.dtype)


def _dil_attention(qk, q_off, k_off, vbuf, v_off, batch, seq, heads):
    assert seq % DIL_CHUNK == 0
    qk3 = qk.reshape(batch, seq, -1)
    v3 = vbuf.reshape(batch, seq, -1)
    blk = (1, DIL_CHUNK, HEAD_DIM)
    cur = lambda off: pl.BlockSpec(blk, lambda b, h, c: (b, c, off + h))
    prev = lambda off: pl.BlockSpec(blk, lambda b, h, c: (b, jnp.maximum(c - 1, 0), off + h))
    return pl.pallas_call(
        functools.partial(_dil_kernel, scale=HEAD_DIM ** -0.5 * LOG2E),
        grid=(batch, heads, seq // DIL_CHUNK),
        in_specs=[cur(q_off), prev(k_off), cur(k_off), prev(v_off), cur(v_off)],
        out_specs=pl.BlockSpec(blk, lambda b, h, c: (b, c, h)),
        out_shape=jax.ShapeDtypeStruct((batch, seq, heads * HEAD_DIM), BF16),
        scratch_shapes=[pltpu.VMEM((len(DIL_BRANCHES), DIL_CHUNK, LANE), F32)] * 3,
        compiler_params=_cparams(("parallel", "parallel", "arbitrary")),
        name="dil_attn",
    )(qk3, qk3, qk3, v3, v3)


def _rope_tables(seq):
    pos = jnp.arange(seq, dtype=jnp.int32).astype(F32)

    def cs(width):
        half = width // 2
        inv = ROPE_THETA ** (-jnp.arange(half, dtype=F32) / half)
        ang = pos[:, None] * inv[None, :]
        return jnp.cos(ang), jnp.sin(ang)

    cos, sin = cs(HEAD_DIM)
    full = (jnp.concatenate([cos, cos], axis=-1), jnp.concatenate([-sin, sin], axis=-1))
    cos, sin = cs(MLA_ROPE)
    z = jnp.zeros_like(cos)
    pad = jnp.zeros((seq, LANE - MLA_ROPE), F32)
    mla = (jnp.concatenate([cos, cos, pad], axis=-1),
           jnp.concatenate([-sin, z, pad], axis=-1),
           jnp.concatenate([z, sin, pad], axis=-1))
    return full, mla


def _layer(xf, xb, p_i, lam_init, alpha, tabs, dims, w_in, w_o, mla_q_norm, mla_w_uq, mla_kv_norm, mla_w_ukv,
           diff_lambda, diff_subln, ln_attn_g, ln_attn_b, w_ff1, w_ff2, ln_ff_g, ln_ff_b,
           w_ple_gate, w_ple_proj, ln_ple_g, ln_ple_b, last):
    batch, seq, d_model = dims
    rope_full, rope_mla = tabs
    mix_heads = d_model // HEAD_DIM
    sb_heads = dil_heads = mla_heads = mix_heads // 4
    diff_heads = mix_heads // 8
    q_rank, kv_rank = mla_q_norm.shape[0], mla_kv_norm.shape[0]
    hw = sb_heads * HEAD_DIM
    widths = (hw,) * 3 + (hw,) * 3 + (q_rank, kv_rank, MLA_ROPE) + (diff_heads * 2 * HEAD_DIM,) * 3
    offs = [0]
    for wd in widths:
        offs.append(offs[-1] + wd)
    col = lambda n: w_in[:, offs[n]:offs[n + 1]]
    a_q, a_k, a_v, b_q, b_k, b_v, c_q, c_kv, c_kr, d_q, d_k, d_v = (col(n) for n in range(12))

    w_plain = jnp.concatenate([a_q, a_k, a_v, d_v], axis=1).astype(BF16)
    w_rope_f = jnp.concatenate([b_q, b_k], axis=1).astype(BF16)
    w_rope_h = jnp.concatenate([d_q, d_k], axis=1).astype(BF16)
    c_used = q_rank + kv_rank + LANE
    c_width = -(-c_used // 1024) * 1024
    w_c = jnp.concatenate([c_q, c_kv, c_kr, jnp.zeros((d_model, c_width - c_used + LANE - MLA_ROPE), w_in.dtype),
                           b_v], axis=1).astype(BF16)

    h_plain = _mm(xb, w_plain, BF16)
    h_rope_f = _mm(xb, w_rope_f, F32, rope=rope_full, seq=seq)
    h_rope_h = _mm(xb, w_rope_h, BF16, rope=rope_full, seq=seq)
    h_c = _mm(xb, w_c, F32)

    y_a = _sb_attention(h_plain, batch, seq, sb_heads, 0, sb_heads, 2 * sb_heads)
    y_b = _dil_attention(h_rope_f, 0, dil_heads, h_c, c_width // HEAD_DIM, batch, seq, dil_heads)

    wq = mla_w_uq.reshape(q_rank, mla_heads, MLA_NOPE + MLA_ROPE)
    wq = jnp.pad(wq, ((0, 0), (0, 0), (0, MLA_QK_PAD - MLA_NOPE - MLA_ROPE))).reshape(q_rank, -1).astype(BF16)
    wkv = mla_w_ukv.reshape(kv_rank, mla_heads, MLA_NOPE + MLA_V)
    wk = wkv[:, :, :MLA_NOPE].reshape(kv_rank, -1).astype(BF16)
    wv = wkv[:, :, MLA_NOPE:].reshape(kv_rank, -1).astype(BF16)
    mq, mk, mv = _mla_up(h_c, c_width, mla_q_norm, mla_kv_norm, wq, wk, wv, rope_mla, seq, mla_heads)
    y_c = _mla_attention(mq, mk, mv, batch, seq, mla_heads)

    y_d = _diff_attention(h_rope_h, 0, diff_heads, h_plain, 3 * sb_heads // 2, diff_lambda, diff_subln,
                          lam_init, batch, seq, diff_heads)

    mix = jnp.concatenate([y_a, y_b, y_c, y_d], axis=-1).reshape(batch * seq, -1)
    xf, xb = _mm_ln(mix, w_o.astype(BF16), xf, ln_attn_g, ln_attn_b, alpha)

    u = _mm(xb, w_ff1.astype(BF16), BF16, act="relu2")
    xf, xb = _mm_ln(u, w_ff2.astype(BF16), xf, ln_ff_g, ln_ff_b, alpha)

    xf, xb = _mm_ln(xb, w_ple_gate.astype(BF16), xf, ln_ple_g, ln_ple_b, alpha,
                    ple=(p_i.astype(BF16), w_ple_proj.astype(BF16)), emit_bf16=not last)
    return xf, xb


def kernel(x, p, w_in, w_o, mla_q_norm, mla_w_uq, mla_kv_norm, mla_w_ukv, diff_lambda, diff_subln,
           ln_attn_g, ln_attn_b, w_ff1, w_ff2, ln_ff_g, ln_ff_b, w_ple_gate, w_ple_proj, ln_ple_g, ln_ple_b):
    batch, seq, d_model = x.shape
    depth = w_in.shape[0]
    alpha = (2 * depth) ** 0.25
    tabs = _rope_tables(seq)
    xf = x.reshape(batch * seq, d_model)
    xb = xf.astype(BF16)
    for i in range(depth):
        lam_init = 0.8 - 0.6 * math.exp(-0.3 * i)
        xf, xb = _layer(xf, xb, p[i].reshape(batch * seq, -1), lam_init, alpha, tabs, (batch, seq, d_model),
                        w_in[i], w_o[i], mla_q_norm[i], mla_w_uq[i], mla_kv_norm[i], mla_w_ukv[i],
                        diff_lambda[i], diff_subln[i], ln_attn_g[i], ln_attn_b[i], w_ff1[i], w_ff2[i],
                        ln_ff_g[i], ln_ff_b[i], w_ple_gate[i], w_ple_proj[i], ln_ple_g[i], ln_ple_b[i],
                        last=i == depth - 1)
    return xf.reshape(batch, seq, d_model)
```

```python
import functools
import math

import jax
import jax.numpy as jnp
from jax import lax
from jax.experimental import pallas as pl
from jax.experimental.pallas import tpu as pltpu

F32 = jnp.float32
BF16 = jnp.bfloat16

LANE = 128
HEAD_DIM = 128
QBLOCK = 128
ROPE_THETA = 10000.0
LN_EPS = 1e-5
RMS_EPS = 1e-6
MLA_NOPE = 128
MLA_ROPE = 64
MLA_V = 128
MLA_QK_PAD = 256
DIL_BRANCHES = ((128, 1), (512, 4), (2048, 16))
DIL_CHUNK = QBLOCK * 16
VMEM_LIMIT_BYTES = 56 * 1024 * 1024
SB_EXP_UNDERFLOW = -104.0
LOG2E = 1.4426950408889634
FLASH_TQ = 1024
FLASH_TK = 512
DIL_MERGE_ROWS = 256


def _pick(n, cands):
    for c in cands:
        if n % c == 0:
            return c
    return n


def _cparams(sem):
    return pltpu.CompilerParams(dimension_semantics=sem, vmem_limit_bytes=VMEM_LIMIT_BYTES)


def _dot(a, b):
    return jnp.dot(a, b, preferred_element_type=F32)


def _dot_nt(a, b):
    return lax.dot_general(a, b, (((1,), (1,)), ((), ())), preferred_element_type=F32)


def _mm_kernel(*refs, act, rope):
    if rope:
        x_ref, w_ref, cos_ref, sin_ref, o_ref = refs
    else:
        x_ref, w_ref, o_ref = refs
    acc = _dot(x_ref[...], w_ref[...])
    if act == "relu2":
        acc = jnp.square(jnp.maximum(acc, 0.0))
    if rope:
        cos = cos_ref[...]
        sin = sin_ref[...]
        for c in range(acc.shape[1] // LANE):
            blk = acc[:, c * LANE:(c + 1) * LANE]
            rot = pltpu.roll(blk, LANE // 2, 1)
            o_ref[:, c * LANE:(c + 1) * LANE] = (blk * cos + rot * sin).astype(o_ref.dtype)
    else:
        o_ref[...] = acc.astype(o_ref.dtype)


def _mm(x, w, out_dtype, *, act=None, rope=None, seq=None):
    M, K = x.shape
    N = w.shape[1]
    tm = _pick(M, (1024, 512, 256, 128))
    tn = _pick(N, (1024, 512, 256, 128))
    in_specs = [pl.BlockSpec((tm, K), lambda i, j: (i, 0)),
                pl.BlockSpec((K, tn), lambda i, j: (0, j))]
    args = [x, w]
    if rope is not None:
        tm = _pick(seq, (tm, 512, 256, 128))
        in_specs[0] = pl.BlockSpec((tm, K), lambda i, j: (i, 0))
        nsb = seq // tm
        tab = pl.BlockSpec((tm, LANE), lambda i, j: (i % nsb, 0))
        in_specs += [tab, tab]
        args += list(rope)
    return pl.pallas_call(
        functools.partial(_mm_kernel, act=act, rope=rope is not None),
        grid=(M // tm, N // tn),
        in_specs=in_specs,
        out_specs=pl.BlockSpec((tm, tn), lambda i, j: (i, j)),
        out_shape=jax.ShapeDtypeStruct((M, N), out_dtype),
        compiler_params=_cparams(("parallel", "parallel")),
        name="proj_mm",
    )(*args)


LN_ROWS = 32
LN_CHAINS = 2
LN_COLS = 1024


def _mm_ln_kernel(*refs, nk, alpha, ple, emit_bf16):
    refs = list(refs)
    x_ref, w_ref, r_ref, g_ref, b_ref = refs[:5]
    rest = refs[5:]
    if ple:
        p_ref, wp_ref = rest[:2]
        rest = rest[2:]
    o_ref = rest[0]
    o16_ref = rest[1] if emit_bf16 else None
    k = pl.program_id(1)
    n_total = o_ref.shape[1]
    nc = _pick(n_total, (LN_COLS, LANE))

    @pl.when(k == 0)
    def _():
        o_ref[...] = jnp.zeros_like(o_ref)

    x = x_ref[...]
    for c in range(n_total // nc):
        cols = slice(c * nc, (c + 1) * nc)
        o_ref[:, cols] += _dot(x, w_ref[:, cols])

    @pl.when(k == nk - 1)
    def _():
        g = g_ref[...]
        b = b_ref[...]

        def body(c, carry):
            chunks = [pl.ds(pl.multiple_of((c * LN_CHAINS + u) * LN_ROWS, LN_ROWS), LN_ROWS)
                      for u in range(LN_CHAINS)]
            outs = [norm_rows(rows) for rows in chunks]
            for rows, out in zip(chunks, outs):
                o_ref[rows, :] = out
                if emit_bf16:
                    o16_ref[rows, :] = out.astype(BF16)
            return carry

        def norm_rows(rows):
            f = o_ref[rows, :]
            if ple:
                f = (0.5 * jnp.tanh(0.5 * f) + 0.5) * _dot(p_ref[rows, :], wp_ref[...])
            y = alpha * r_ref[rows, :] + f
            mu = jnp.mean(y, axis=-1, keepdims=True)
            d = y - mu
            var = jnp.mean(d * d, axis=-1, keepdims=True)
            return d * lax.rsqrt(var + LN_EPS) * g + b

        lax.fori_loop(0, o_ref.shape[0] // (LN_ROWS * LN_CHAINS), body, 0)


def _mm_ln(x, w, resid, g, b, alpha, *, ple=None, emit_bf16=True):
    M, K = x.shape
    N = w.shape[1]
    tm = _pick(M, (512, 256, 128))
    tk = _pick(K, (512, 256, 128))
    nk = K // tk
    in_specs = [pl.BlockSpec((tm, tk), lambda i, k: (i, k)),
                pl.BlockSpec((tk, N), lambda i, k: (k, 0)),
                pl.BlockSpec((tm, N), lambda i, k: (i, 0)),
                pl.BlockSpec((1, N), lambda i, k: (0, 0)),
                pl.BlockSpec((1, N), lambda i, k: (0, 0))]
    args = [x, w, resid, g.reshape(1, N), b.reshape(1, N)]
    if ple is not None:
        pp, wp = ple
        in_specs += [pl.BlockSpec((tm, pp.shape[1]), lambda i, k: (i, 0)),
                     pl.BlockSpec(wp.shape, lambda i, k: (0, 0))]
        args += [pp, wp]
    out_specs = [pl.BlockSpec((tm, N), lambda i, k: (i, 0))]
    out_shape = [jax.ShapeDtypeStruct((M, N), F32)]
    if emit_bf16:
        out_specs.append(pl.BlockSpec((tm, N), lambda i, k: (i, 0)))
        out_shape.append(jax.ShapeDtypeStruct((M, N), BF16))
    outs = pl.pallas_call(
        functools.partial(_mm_ln_kernel, nk=nk, alpha=alpha, ple=ple is not None, emit_bf16=emit_bf16),
        grid=(M // tm, nk),
        in_specs=in_specs,
        out_specs=out_specs,
        out_shape=out_shape,
        compiler_params=_cparams(("parallel", "arbitrary")),
        name="mm_res_ln",
    )(*args)
    return (outs[0], outs[1]) if emit_bf16 else (outs[0], None)


def _rope64(xr, cos, s_lo, s_hi):
    return xr * cos + pltpu.roll(xr, LANE - MLA_ROPE // 2, 1) * s_lo + pltpu.roll(xr, MLA_ROPE // 2, 1) * s_hi


def _mla_up_kernel(c_ref, gq_ref, gkv_ref, wq_ref, wk_ref, wv_ref, cos_ref, slo_ref, shi_ref,
                   q_ref, k_ref, v_ref, *, q_rank, kv_rank, heads):
    cos = cos_ref[...]
    s_lo = slo_ref[...]
    s_hi = shi_ref[...]

    def rms(t, g):
        return (t * lax.rsqrt(jnp.mean(t * t, axis=-1, keepdims=True) + RMS_EPS) * g).astype(BF16)

    nq = rms(c_ref[:, :q_rank], gq_ref[...])
    qf = _dot(nq, wq_ref[...])
    for h in range(heads):
        lo = h * MLA_QK_PAD
        q_ref[:, lo:lo + MLA_NOPE] = qf[:, lo:lo + MLA_NOPE].astype(BF16)
        q_ref[:, lo + MLA_NOPE:lo + MLA_QK_PAD] = _rope64(
            qf[:, lo + MLA_NOPE:lo + MLA_QK_PAD], cos, s_lo, s_hi).astype(BF16)
    nkv = rms(c_ref[:, q_rank:q_rank + kv_rank], gkv_ref[...])
    kf = _dot(nkv, wk_ref[...])
    v_ref[...] = _dot(nkv, wv_ref[...]).astype(BF16)
    kr = _rope64(c_ref[:, q_rank + kv_rank:q_rank + kv_rank + LANE], cos, s_lo, s_hi).astype(BF16)
    for h in range(heads):
        lo = h * MLA_QK_PAD
        k_ref[:, lo:lo + MLA_NOPE] = kf[:, h * MLA_NOPE:(h + 1) * MLA_NOPE].astype(BF16)
        k_ref[:, lo + MLA_NOPE:lo + MLA_QK_PAD] = kr


def _mla_up(cbuf, c_width, gq, gkv, wq, wk, wv, tabs, seq, heads):
    M = cbuf.shape[0]
    q_rank, kv_rank = gq.shape[0], gkv.shape[0]
    tm = _pick(seq, (512, 256, 128))
    nsb = seq // tm
    const = lambda shape: pl.BlockSpec(shape, lambda i: (0, 0))
    tab = pl.BlockSpec((tm, LANE), lambda i: (i % nsb, 0))
    row = lambda n: pl.BlockSpec((tm, n), lambda i: (i, 0))
    return pl.pallas_call(
        functools.partial(_mla_up_kernel, q_rank=q_rank, kv_rank=kv_rank, heads=heads),
        grid=(M // tm,),
        in_specs=[row(c_width), const((1, q_rank)), const((1, kv_rank)),
                  const(wq.shape), const(wk.shape), const(wv.shape), tab, tab, tab],
        out_specs=[row(heads * MLA_QK_PAD), row(heads * MLA_QK_PAD), row(heads * MLA_V)],
        out_shape=[jax.ShapeDtypeStruct((M, heads * MLA_QK_PAD), BF16),
                   jax.ShapeDtypeStruct((M, heads * MLA_QK_PAD), BF16),
                   jax.ShapeDtypeStruct((M, heads * MLA_V), BF16)],
        compiler_params=_cparams(("parallel",)),
        name="mla_up",
    )(cbuf, gq.reshape(1, -1), gkv.reshape(1, -1), wq, wk, wv, *tabs)


def _attn_specs(tq, seq, dq, dk, dv, q_off, k_off, v_off):
    return [pl.BlockSpec((1, tq, dq), lambda b, h, i: (b, i, q_off + h)),
            pl.BlockSpec((1, seq, dk), lambda b, h, i: (b, 0, k_off + h)),
            pl.BlockSpec((1, seq, dv), lambda b, h, i: (b, 0, v_off + h))]


def _sb_kernel(q_ref, k_ref, v_ref, tri_ref, o_ref, *, t, scale):
    i = pl.program_id(2)
    q = q_ref[0]
    tri = tri_ref[...]
    row = lax.broadcasted_iota(jnp.int32, (t, t), 0)
    col = lax.broadcasted_iota(jnp.int32, (t, t), 1)
    past = col < row

    def block(j, c, acc, keep):
        start = pl.multiple_of(j * t, t)
        ks = k_ref[0, pl.ds(start, t), :]
        vs = v_ref[0, pl.ds(start, t), :]
        z = _dot_nt(q, ks) * scale
        log_beta = jnp.minimum(z, 0.0) - jnp.log1p(jnp.exp(-jnp.abs(z)))
        log_1m = log_beta - z
        if keep is not None:
            log_1m = jnp.where(keep, log_1m, 0.0)
        hi = log_1m.astype(BF16)
        lo = (log_1m - hi.astype(F32)).astype(BF16)
        between = _dot(hi, tri) + _dot(lo, tri) + c
        w = jnp.exp(log_beta + between)
        if keep is not None:
            w = jnp.where(keep, w, 0.0)
        acc = acc + _dot(w.astype(BF16), vs)
        c = c + jnp.sum(log_1m, axis=-1, keepdims=True)
        return c, acc

    c0 = jnp.zeros((t, 1), F32)
    acc0 = jnp.zeros((t, v_ref.shape[2]), F32)
    c, acc = block(i, c0, acc0, past)
    c, acc = block(jnp.maximum(i - 1, 0), c, acc, row < jnp.where(i > 0, t, 0))

    def cond(carry):
        jj, c_max, _, _ = carry
        return jnp.logical_and(jj < i, c_max > SB_EXP_UNDERFLOW)

    def body(carry):
        jj, _, c, acc = carry
        c, acc = block(i - 1 - jj, c, acc, None)
        return jj + 1, jnp.max(c), c, acc

    _, _, c, acc = lax.while_loop(cond, body, (jnp.int32(1), jnp.max(c), c, acc))
    o_ref[0] = acc.astype(o_ref.dtype)


def _sb_attention(qkv, batch, seq, heads, q_off, k_off, v_off):
    t = _pick(seq, (256, 128))
    arr = qkv.reshape(batch, seq, -1)
    tri = (lax.broadcasted_iota(jnp.int32, (t, t), 0) > lax.broadcasted_iota(jnp.int32, (t, t), 1)).astype(BF16)
    specs = _attn_specs(t, seq, HEAD_DIM, HEAD_DIM, HEAD_DIM, q_off, k_off, v_off)
    specs.append(pl.BlockSpec((t, t), lambda b, h, i: (0, 0)))
    return pl.pallas_call(
        functools.partial(_sb_kernel, t=t, scale=HEAD_DIM ** -0.5),
        grid=(batch, heads, seq // t),
        in_specs=specs,
        out_specs=pl.BlockSpec((1, t, HEAD_DIM), lambda b, h, i: (b, i, h)),
        out_shape=jax.ShapeDtypeStruct((batch, seq, heads * HEAD_DIM), BF16),
        compiler_params=_cparams(("parallel", "parallel", "arbitrary")),
        name="sb_attn",
    )(arr, arr, arr, tri)


def _softmax_pv(s, vs, m, l, acc, mask):
    if mask is not None:
        s = jnp.where(mask, s, -jnp.inf)
    m_new = jnp.maximum(m, jnp.max(s, axis=-1, keepdims=True))
    a = jnp.exp2(m - m_new)
    p = jnp.exp2(s - m_new)
    l = a * l + jnp.sum(p, axis=-1, keepdims=True)
    acc = a * acc + _dot(p.astype(BF16), vs)
    return m_new, l, acc


def _flash_init(t, dv):
    return jnp.full((t, 1), -jnp.inf, F32), jnp.zeros((t, 1), F32), jnp.zeros((t, dv), F32)


def _causal_sweep(i, tq, tk, step, init):
    nsub = tq // tk
    row = lax.broadcasted_iota(jnp.int32, (tq, tk), 0)
    col = lax.broadcasted_iota(jnp.int32, (tq, tk), 1)
    def body(jj, c):
        for u in range(nsub):
            c = step(jj * nsub + u, c, None)
        return c

    carry = lax.fori_loop(0, i, body, init)
    for u in range(nsub):
        carry = step(i * nsub + u, carry, col + u * tk <= row)
    return carry


def _flash_tiles(seq):
    tq = _pick(seq, (FLASH_TQ, 512, 256, 128))
    return tq, _pick(tq, (FLASH_TK, 256, 128))


def _mla_kernel(q_ref, k_ref, v_ref, o_ref, *, tq, tk, scale):
    q = q_ref[0]

    def step(j, carry, mask):
        rows = pl.ds(pl.multiple_of(j * tk, tk), tk)
        s = _dot_nt(q, k_ref[0, rows, :]) * scale
        return _softmax_pv(s, v_ref[0, rows, :], *carry, mask)

    m, l, acc = _causal_sweep(pl.program_id(2), tq, tk, step, _flash_init(tq, v_ref.shape[2]))
    o_ref[0] = (acc / l).astype(o_ref.dtype)


def _mla_attention(q, k, v, batch, seq, heads):
    t, tk = _flash_tiles(seq)
    return pl.pallas_call(
        functools.partial(_mla_kernel, tq=t, tk=tk, scale=(MLA_NOPE + MLA_ROPE) ** -0.5 * LOG2E),
        grid=(batch, heads, seq // t),
        in_specs=_attn_specs(t, seq, MLA_QK_PAD, MLA_QK_PAD, MLA_V, 0, 0, 0),
        out_specs=pl.BlockSpec((1, t, MLA_V), lambda b, h, i: (b, i, h)),
        out_shape=jax.ShapeDtypeStruct((batch, seq, heads * MLA_V), BF16),
        compiler_params=_cparams(("parallel", "parallel", "arbitrary")),
        name="mla_attn",
    )(q.reshape(batch, seq, -1), k.reshape(batch, seq, -1), v.reshape(batch, seq, -1))


def _diff_kernel(q_ref, k_ref, v_ref, lam_ref, g_ref, o_ref, *, tq, tk, scale, lam_init):
    d = HEAD_DIM
    q1 = q_ref[0, :, :d]
    q2 = q_ref[0, :, d:]

    def step(j, carry, mask):
        rows = pl.ds(pl.multiple_of(j * tk, tk), tk)
        vs = v_ref[0, rows, :]
        s1 = _dot_nt(q1, k_ref[0, rows, :d]) * scale
        s2 = _dot_nt(q2, k_ref[0, rows, d:]) * scale
        return _softmax_pv(s1, vs, *carry[:3], mask) + _softmax_pv(s2, vs, *carry[3:], mask)

    one = _flash_init(tq, v_ref.shape[2])
    m1, l1, a1, m2, l2, a2 = _causal_sweep(pl.program_id(2), tq, tk, step, one + one)
    lp = lam_ref[...]
    lam = (jnp.exp(jnp.sum(lp[0:1] * lp[1:2], axis=-1, keepdims=True))
           - jnp.exp(jnp.sum(lp[2:3] * lp[3:4], axis=-1, keepdims=True)) + lam_init)
    o = a1 / l1 - lam * (a2 / l2)
    o = o * lax.rsqrt(jnp.mean(o * o, axis=-1, keepdims=True) + RMS_EPS) * g_ref[...]
    o_ref[0] = (o * (1.0 - lam_init)).astype(o_ref.dtype)


def _diff_attention(qk, q_off, k_off, vbuf, v_off, lam_params, subln_g, lam_init, batch, seq, heads):
    t, tk = _flash_tiles(seq)
    dd = 2 * HEAD_DIM
    specs = [pl.BlockSpec((1, t, dd), lambda b, h, i: (b, i, q_off + h)),
             pl.BlockSpec((1, seq, dd), lambda b, h, i: (b, 0, k_off + h)),
             pl.BlockSpec((1, seq, dd), lambda b, h, i: (b, 0, v_off + h)),
             pl.BlockSpec(lam_params.shape, lambda b, h, i: (0, 0)),
             pl.BlockSpec((1, dd), lambda b, h, i: (0, 0))]
    return pl.pallas_call(
        functools.partial(_diff_kernel, tq=t, tk=tk, scale=HEAD_DIM ** -0.5 * LOG2E, lam_init=lam_init),
        grid=(batch, heads, seq // t),
        in_specs=specs,
        out_specs=pl.BlockSpec((1, t, dd), lambda b, h, i: (b, i, h)),
        out_shape=jax.ShapeDtypeStruct((batch, seq, heads * dd), BF16),
        compiler_params=_cparams(("parallel", "parallel", "arbitrary")),
        name="diff_attn",
    )(qk.reshape(batch, seq, -1), qk.reshape(batch, seq, -1), vbuf.reshape(batch, seq, -1),
      lam_params, subln_g.reshape(1, dd))


def _dil_kernel(q_ref, kp_ref, kc_ref, vp_ref, vc_ref, o_ref, acc_scr, m_scr, l_scr, *, scale):
    has_prev = pl.program_id(2) > 0
    ri = lax.broadcasted_iota(jnp.int32, (QBLOCK, QBLOCK), 0)
    ci = lax.broadcasted_iota(jnp.int32, (QBLOCK, QBLOCK), 1)
    cur_ok = ci <= ri
    prev_ok = ci >= ri
    prev_ok_first = ci >= ri + jnp.where(has_prev, 0, QBLOCK)

    def rows(n, r, dil):
        start = n * QBLOCK * dil + r
        return pl.ds(start, QBLOCK) if dil == 1 else pl.ds(start, QBLOCK, stride=dil)

    for bi, (window, dil) in enumerate(DIL_BRANCHES):
        assert window // dil == QBLOCK
        nblk = DIL_CHUNK // (QBLOCK * dil)
        for r in range(dil):
            kv_prev = None
            for n in range(nblk):
                cur = rows(n, r, dil)
                q = q_ref[0, cur, :].astype(BF16)
                kc = kc_ref[0, cur, :].astype(BF16)
                vc = vc_ref[0, cur, :].astype(BF16)
                if n > 0:
                    kp, vp = kv_prev
                    pmask = prev_ok
                else:
                    prv = rows(nblk - 1, r, dil)
                    kp = kp_ref[0, prv, :].astype(BF16)
                    vp = vp_ref[0, prv, :].astype(BF16)
                    pmask = prev_ok_first
                kv_prev = (kc, vc)
                s_p = jnp.where(pmask, _dot_nt(q, kp) * scale, -jnp.inf)
                s_c = jnp.where(cur_ok, _dot_nt(q, kc) * scale, -jnp.inf)
                mx = jnp.maximum(jnp.max(s_p, axis=-1, keepdims=True), jnp.max(s_c, axis=-1, keepdims=True))
                mx = jnp.broadcast_to(mx, (QBLOCK, LANE))
                e_p = jnp.exp2(s_p - mx)
                e_c = jnp.exp2(s_c - mx)
                lsum = jnp.sum(e_p, axis=-1, keepdims=True) + jnp.sum(e_c, axis=-1, keepdims=True)
                acc_scr[bi, cur, :] = _dot(e_p.astype(BF16), vp) + _dot(e_c.astype(BF16), vc)
                m_scr[bi, cur, :] = mx
                l_scr[bi, cur, :] = jnp.broadcast_to(lsum, (QBLOCK, LANE))

    nb = len(DIL_BRANCHES)
    for c in range(DIL_CHUNK // DIL_MERGE_ROWS):
        rs = pl.ds(c * DIL_MERGE_ROWS, DIL_MERGE_ROWS)
        ms = [m_scr[b, rs, :] for b in range(nb)]
        top = functools.reduce(jnp.maximum, ms)
        ws = [jnp.exp2(m - top) for m in ms]
        num = sum(w * acc_scr[b, rs, :] for b, w in enumerate(ws))
        den = sum(w * l_scr[b, rs, :] for b, w in enumerate(ws))
        o_ref[0, rs, :] = (num / den).astype(o_ref.dtype)


def _dil_attention(qk, q_off, k_off, vbuf, v_off, batch, seq, heads):
    assert seq % DIL_CHUNK == 0
    qk3 = qk.reshape(batch, seq, -1)
    v3 = vbuf.reshape(batch, seq, -1)
    blk = (1, DIL_CHUNK, HEAD_DIM)
    cur = lambda off: pl.BlockSpec(blk, lambda b, h, c: (b, c, off + h))
    prev = lambda off: pl.BlockSpec(blk, lambda b, h, c: (b, jnp.maximum(c - 1, 0), off + h))
    return pl.pallas_call(
        functools.partial(_dil_kernel, scale=HEAD_DIM ** -0.5 * LOG2E),
        grid=(batch, heads, seq // DIL_CHUNK),
        in_specs=[cur(q_off), prev(k_off), cur(k_off), prev(v_off), cur(v_off)],
        out_specs=pl.BlockSpec(blk, lambda b, h, c: (b, c, h)),
        out_shape=jax.ShapeDtypeStruct((batch, seq, heads * HEAD_DIM), BF16),
        scratch_shapes=[pltpu.VMEM((len(DIL_BRANCHES), DIL_CHUNK, LANE), F32)] * 3,
        compiler_params=_cparams(("parallel", "parallel", "arbitrary")),
        name="dil_attn",
    )(qk3, qk3, qk3, v3, v3)


def _rope_tables(seq):
    pos = jnp.arange(seq, dtype=jnp.int32).astype(F32)

    def cs(width):
        half = width // 2
        inv = ROPE_THETA ** (-jnp.arange(half, dtype=F32) / half)
        ang = pos[:, None] * inv[None, :]
        return jnp.cos(ang), jnp.sin(ang)

    cos, sin = cs(HEAD_DIM)
    full = (jnp.concatenate([cos, cos], axis=-1), jnp.concatenate([-sin, sin], axis=-1))
    cos, sin = cs(MLA_ROPE)
    z = jnp.zeros_like(cos)
    pad = jnp.zeros((seq, LANE - MLA_ROPE), F32)
    mla = (jnp.concatenate([cos, cos, pad], axis=-1),
           jnp.concatenate([-sin, z, pad], axis=-1),
           jnp.concatenate([z, sin, pad], axis=-1))
    return full, mla


def _layer(xf, xb, p_i, lam_init, alpha, tabs, dims, w_in, w_o, mla_q_norm, mla_w_uq, mla_kv_norm, mla_w_ukv,
           diff_lambda, diff_subln, ln_attn_g, ln_attn_b, w_ff1, w_ff2, ln_ff_g, ln_ff_b,
           w_ple_gate, w_ple_proj, ln_ple_g, ln_ple_b, last):
    batch, seq, d_model = dims
    rope_full, rope_mla = tabs
    mix_heads = d_model // HEAD_DIM
    sb_heads = dil_heads = mla_heads = mix_heads // 4
    diff_heads = mix_heads // 8
    q_rank, kv_rank = mla_q_norm.shape[0], mla_kv_norm.shape[0]
    hw = sb_heads * HEAD_DIM
    widths = (hw,) * 3 + (hw,) * 3 + (q_rank, kv_rank, MLA_ROPE) + (diff_heads * 2 * HEAD_DIM,) * 3
    offs = [0]
    for wd in widths:
        offs.append(offs[-1] + wd)
    col = lambda n: w_in[:, offs[n]:offs[n + 1]]
    a_q, a_k, a_v, b_q, b_k, b_v, c_q, c_kv, c_kr, d_q, d_k, d_v = (col(n) for n in range(12))

    w_plain = jnp.concatenate([a_q, a_k, a_v, d_v], axis=1).astype(BF16)
    w_rope_f = jnp.concatenate([b_q, b_k], axis=1).astype(BF16)
    w_rope_h = jnp.concatenate([d_q, d_k], axis=1).astype(BF16)
    c_used = q_rank + kv_rank + LANE
    c_width = -(-c_used // 1024) * 1024
    w_c = jnp.concatenate([c_q, c_kv, c_kr, jnp.zeros((d_model, c_width - c_used + LANE - MLA_ROPE), w_in.dtype),
                           b_v], axis=1).astype(BF16)

    h_plain = _mm(xb, w_plain, BF16)
    h_rope_f = _mm(xb, w_rope_f, F32, rope=rope_full, seq=seq)
    h_rope_h = _mm(xb, w_rope_h, BF16, rope=rope_full, seq=seq)
    h_c = _mm(xb, w_c, F32)

    y_a = _sb_attention(h_plain, batch, seq, sb_heads, 0, sb_heads, 2 * sb_heads)
    y_b = _dil_attention(h_rope_f, 0, dil_heads, h_c, c_width // HEAD_DIM, batch, seq, dil_heads)

    wq = mla_w_uq.reshape(q_rank, mla_heads, MLA_NOPE + MLA_ROPE)
    wq = jnp.pad(wq, ((0, 0), (0, 0), (0, MLA_QK_PAD - MLA_NOPE - MLA_ROPE))).reshape(q_rank, -1).astype(BF16)
    wkv = mla_w_ukv.reshape(kv_rank, mla_heads, MLA_NOPE + MLA_V)
    wk = wkv[:, :, :MLA_NOPE].reshape(kv_rank, -1).astype(BF16)
    wv = wkv[:, :, MLA_NOPE:].reshape(kv_rank, -1).astype(BF16)
    mq, mk, mv = _mla_up(h_c, c_width, mla_q_norm, mla_kv_norm, wq, wk, wv, rope_mla, seq, mla_heads)
    y_c = _mla_attention(mq, mk, mv, batch, seq, mla_heads)

    y_d = _diff_attention(h_rope_h, 0, diff_heads, h_plain, 3 * sb_heads // 2, diff_lambda, diff_subln,
                          lam_init, batch, seq, diff_heads)

    mix = jnp.concatenate([y_a, y_b, y_c, y_d], axis=-1).reshape(batch * seq, -1)
    xf, xb = _mm_ln(mix, w_o.astype(BF16), xf, ln_attn_g, ln_attn_b, alpha)

    u = _mm(xb, w_ff1.astype(BF16), BF16, act="relu2")
    xf, xb = _mm_ln(u, w_ff2.astype(BF16), xf, ln_ff_g, ln_ff_b, alpha)

    xf, xb = _mm_ln(xb, w_ple_gate.astype(BF16), xf, ln_ple_g, ln_ple_b, alpha,
                    ple=(p_i.astype(BF16), w_ple_proj.astype(BF16)), emit_bf16=not last)
    return xf, xb


def kernel(x, p, w_in, w_o, mla_q_norm, mla_w_uq, mla_kv_norm, mla_w_ukv, diff_lambda, diff_subln,
           ln_attn_g, ln_attn_b, w_ff1, w_ff2, ln_ff_g, ln_ff_b, w_ple_gate, w_ple_proj, ln_ple_g, ln_ple_b):
    batch, seq, d_model = x.shape
    depth = w_in.shape[0]
    alpha = (2 * depth) ** 0.25
    tabs = _rope_tables(seq)
    xf = x.reshape(batch * seq, d_model)
    xb = xf.astype(BF16)
    for i in range(depth):
        lam_init = 0.8 - 0.6 * math.exp(-0.3 * i)
        xf, xb = _layer(xf, xb, p[i].reshape(batch * seq, -1), lam_init, alpha, tabs, (batch, seq, d_model),
                        w_in[i], w_o[i], mla_q_norm[i], mla_w_uq[i], mla_kv_norm[i], mla_w_ukv[i],
                        diff_lambda[i], diff_subln[i], ln_attn_g[i], ln_attn_b[i], w_ff1[i], w_ff2[i],
                        ln_ff_g[i], ln_ff_b[i], w_ple_gate[i], w_ple_proj[i], ln_ple_g[i], ln_ple_b[i],
                        last=i == depth - 1)
    return xf.reshape(batch, seq, d_model)
```

```python
import functools
import math

import jax
import jax.numpy as jnp
from jax import lax
from jax.experimental import pallas as pl
from jax.experimental.pallas import tpu as pltpu

F32 = jnp.float32
BF16 = jnp.bfloat16

LANE = 128
HEAD_DIM = 128
QBLOCK = 128
ROPE_THETA = 10000.0
LN_EPS = 1e-5
RMS_EPS = 1e-6
MLA_NOPE = 128
MLA_ROPE = 64
MLA_V = 128
MLA_QK_PAD = 256
DIL_BRANCHES = ((128, 1), (512, 4), (2048, 16))
DIL_CHUNK = QBLOCK * 16
VMEM_LIMIT_BYTES = 56 * 1024 * 1024
SB_EXP_UNDERFLOW = -104.0
LOG2E = 1.4426950408889634
FLASH_TQ = 1024
FLASH_TK = 512
DIL_MERGE_ROWS = 256
DIL_UNITS = 16


def _pick(n, cands):
    for c in cands:
        if n % c == 0:
            return c
    return n


def _cparams(sem):
    return pltpu.CompilerParams(dimension_semantics=sem, vmem_limit_bytes=VMEM_LIMIT_BYTES)


def _dot(a, b):
    return jnp.dot(a, b, preferred_element_type=F32)


def _dot_nt(a, b):
    return lax.dot_general(a, b, (((1,), (1,)), ((), ())), preferred_element_type=F32)


def _mm_kernel(*refs, rope):
    if rope:
        x_ref, w_ref, cos_ref, sin_ref, o_ref = refs
    else:
        x_ref, w_ref, o_ref = refs
    acc = _dot(x_ref[...], w_ref[...])
    if rope:
        cos = cos_ref[...]
        sin = sin_ref[...]
        for c in range(acc.shape[1] // LANE):
            blk = acc[:, c * LANE:(c + 1) * LANE]
            rot = pltpu.roll(blk, LANE // 2, 1)
            o_ref[:, c * LANE:(c + 1) * LANE] = (blk * cos + rot * sin).astype(o_ref.dtype)
    else:
        o_ref[...] = acc.astype(o_ref.dtype)


def _mm(x, w, out_dtype, *, rope=None, seq=None):
    M, K = x.shape
    N = w.shape[1]
    tm = _pick(M, (1024, 512, 256, 128))
    tn = _pick(N, (1024, 512, 256, 128))
    in_specs = [pl.BlockSpec((tm, K), lambda i, j: (i, 0)),
                pl.BlockSpec((K, tn), lambda i, j: (0, j))]
    args = [x, w]
    if rope is not None:
        tm = _pick(seq, (tm, 512, 256, 128))
        in_specs[0] = pl.BlockSpec((tm, K), lambda i, j: (i, 0))
        nsb = seq // tm
        tab = pl.BlockSpec((tm, LANE), lambda i, j: (i % nsb, 0))
        in_specs += [tab, tab]
        args += list(rope)
    return pl.pallas_call(
        functools.partial(_mm_kernel, rope=rope is not None),
        grid=(M // tm, N // tn),
        in_specs=in_specs,
        out_specs=pl.BlockSpec((tm, tn), lambda i, j: (i, j)),
        out_shape=jax.ShapeDtypeStruct((M, N), out_dtype),
        compiler_params=_cparams(("parallel", "parallel")),
        name="proj_mm",
    )(*args)


WS_CAST_ROWS = 512


def _mm_ws_kernel(x_ref, w_ref, o_ref, wb_ref):
    @pl.when(pl.program_id(1) == 0)
    def _():
        for r in range(0, w_ref.shape[0], WS_CAST_ROWS):
            wb_ref[r:r + WS_CAST_ROWS, :] = w_ref[r:r + WS_CAST_ROWS, :].astype(BF16)

    acc = _dot(x_ref[...], wb_ref[...])
    o_ref[...] = jnp.square(jnp.maximum(acc, 0.0)).astype(o_ref.dtype)


def _mm_ws_relu2(x, w):
    M, K = x.shape
    N = w.shape[1]
    tm = _pick(M, (512, 256, 128))
    tn = _pick(N, (1024, 512, 256, 128))
    assert K % WS_CAST_ROWS == 0
    return pl.pallas_call(
        _mm_ws_kernel,
        grid=(N // tn, M // tm),
        in_specs=[pl.BlockSpec((tm, K), lambda j, i: (i, 0)),
                  pl.BlockSpec((K, tn), lambda j, i: (0, j))],
        out_specs=pl.BlockSpec((tm, tn), lambda j, i: (i, j)),
        out_shape=jax.ShapeDtypeStruct((M, N), BF16),
        scratch_shapes=[pltpu.VMEM((K, tn), BF16)],
        compiler_params=_cparams(("arbitrary", "arbitrary")),
        name="ff1_mm",
    )(x, w)


LN_ROWS = 32
LN_CHAINS = 2
LN_COLS = 1024


def _mm_ln_kernel(*refs, nk, alpha, ple, emit_bf16):
    refs = list(refs)
    x_ref, w_ref, r_ref, g_ref, b_ref = refs[:5]
    rest = refs[5:]
    if ple:
        p_ref, wp_ref = rest[:2]
        rest = rest[2:]
    o_ref = rest[0]
    o16_ref = rest[1] if emit_bf16 else None
    k = pl.program_id(1)
    n_total = o_ref.shape[1]
    nc = _pick(n_total, (LN_COLS, LANE))

    @pl.when(k == 0)
    def _():
        o_ref[...] = jnp.zeros_like(o_ref)

    x = x_ref[...]
    for c in range(n_total // nc):
        cols = slice(c * nc, (c + 1) * nc)
        o_ref[:, cols] += _dot(x, w_ref[:, cols])

    @pl.when(k == nk - 1)
    def _():
        g = g_ref[...]
        b = b_ref[...]

        def body(c, carry):
            chunks = [pl.ds(pl.multiple_of((c * LN_CHAINS + u) * LN_ROWS, LN_ROWS), LN_ROWS)
                      for u in range(LN_CHAINS)]
            outs = [norm_rows(rows) for rows in chunks]
            for rows, out in zip(chunks, outs):
                o_ref[rows, :] = out
                if emit_bf16:
                    o16_ref[rows, :] = out.astype(BF16)
            return carry

        def norm_rows(rows):
            f = o_ref[rows, :]
            if ple:
                f = (0.5 * jnp.tanh(0.5 * f) + 0.5) * _dot(p_ref[rows, :], wp_ref[...])
            y = alpha * r_ref[rows, :] + f
            mu = jnp.mean(y, axis=-1, keepdims=True)
            d = y - mu
            var = jnp.mean(d * d, axis=-1, keepdims=True)
            return d * lax.rsqrt(var + LN_EPS) * g + b

        lax.fori_loop(0, o_ref.shape[0] // (LN_ROWS * LN_CHAINS), body, 0)


def _mm_ln(x, w, resid, g, b, alpha, *, ple=None, emit_bf16=True):
    M, K = x.shape
    N = w.shape[1]
    tm = _pick(M, (512, 256, 128))
    tk = _pick(K, (512, 256, 128))
    nk = K // tk
    in_specs = [pl.BlockSpec((tm, tk), lambda i, k: (i, k)),
                pl.BlockSpec((tk, N), lambda i, k: (k, 0)),
                pl.BlockSpec((tm, N), lambda i, k: (i, 0)),
                pl.BlockSpec((1, N), lambda i, k: (0, 0)),
                pl.BlockSpec((1, N), lambda i, k: (0, 0))]
    args = [x, w, resid, g.reshape(1, N), b.reshape(1, N)]
    if ple is not None:
        pp, wp = ple
        in_specs += [pl.BlockSpec((tm, pp.shape[1]), lambda i, k: (i, 0)),
                     pl.BlockSpec(wp.shape, lambda i, k: (0, 0))]
        args += [pp, wp]
    out_specs = [pl.BlockSpec((tm, N), lambda i, k: (i, 0))]
    out_shape = [jax.ShapeDtypeStruct((M, N), F32)]
    if emit_bf16:
        out_specs.append(pl.BlockSpec((tm, N), lambda i, k: (i, 0)))
        out_shape.append(jax.ShapeDtypeStruct((M, N), BF16))
    outs = pl.pallas_call(
        functools.partial(_mm_ln_kernel, nk=nk, alpha=alpha, ple=ple is not None, emit_bf16=emit_bf16),
        grid=(M // tm, nk),
        in_specs=in_specs,
        out_specs=out_specs,
        out_shape=out_shape,
        compiler_params=_cparams(("parallel", "arbitrary")),
        name="mm_res_ln",
    )(*args)
    return (outs[0], outs[1]) if emit_bf16 else (outs[0], None)


def _rope64(xr, cos, s_lo, s_hi):
    return xr * cos + pltpu.roll(xr, LANE - MLA_ROPE // 2, 1) * s_lo + pltpu.roll(xr, MLA_ROPE // 2, 1) * s_hi


def _mla_up_kernel(c_ref, gq_ref, gkv_ref, wq_ref, wk_ref, wv_ref, cos_ref, slo_ref, shi_ref,
                   q_ref, k_ref, v_ref, *, q_rank, kv_rank, heads):
    cos = cos_ref[...]
    s_lo = slo_ref[...]
    s_hi = shi_ref[...]

    def rms(t, g):
        return (t * lax.rsqrt(jnp.mean(t * t, axis=-1, keepdims=True) + RMS_EPS) * g).astype(BF16)

    nq = rms(c_ref[:, :q_rank], gq_ref[...])
    qf = _dot(nq, wq_ref[...])
    for h in range(heads):
        lo = h * MLA_QK_PAD
        q_ref[:, lo:lo + MLA_NOPE] = qf[:, lo:lo + MLA_NOPE].astype(BF16)
        q_ref[:, lo + MLA_NOPE:lo + MLA_QK_PAD] = _rope64(
            qf[:, lo + MLA_NOPE:lo + MLA_QK_PAD], cos, s_lo, s_hi).astype(BF16)
    nkv = rms(c_ref[:, q_rank:q_rank + kv_rank], gkv_ref[...])
    kf = _dot(nkv, wk_ref[...])
    v_ref[...] = _dot(nkv, wv_ref[...]).astype(BF16)
    kr = _rope64(c_ref[:, q_rank + kv_rank:q_rank + kv_rank + LANE], cos, s_lo, s_hi).astype(BF16)
    for h in range(heads):
        lo = h * MLA_QK_PAD
        k_ref[:, lo:lo + MLA_NOPE] = kf[:, h * MLA_NOPE:(h + 1) * MLA_NOPE].astype(BF16)
        k_ref[:, lo + MLA_NOPE:lo + MLA_QK_PAD] = kr


def _mla_up(cbuf, c_width, gq, gkv, wq, wk, wv, tabs, seq, heads):
    M = cbuf.shape[0]
    q_rank, kv_rank = gq.shape[0], gkv.shape[0]
    tm = _pick(seq, (512, 256, 128))
    nsb = seq // tm
    const = lambda shape: pl.BlockSpec(shape, lambda i: (0, 0))
    tab = pl.BlockSpec((tm, LANE), lambda i: (i % nsb, 0))
    row = lambda n: pl.BlockSpec((tm, n), lambda i: (i, 0))
    return pl.pallas_call(
        functools.partial(_mla_up_kernel, q_rank=q_rank, kv_rank=kv_rank, heads=heads),
        grid=(M // tm,),
        in_specs=[row(c_width), const((1, q_rank)), const((1, kv_rank)),
                  const(wq.shape), const(wk.shape), const(wv.shape), tab, tab, tab],
        out_specs=[row(heads * MLA_QK_PAD), row(heads * MLA_QK_PAD), row(heads * MLA_V)],
        out_shape=[jax.ShapeDtypeStruct((M, heads * MLA_QK_PAD), BF16),
                   jax.ShapeDtypeStruct((M, heads * MLA_QK_PAD), BF16),
                   jax.ShapeDtypeStruct((M, heads * MLA_V), BF16)],
        compiler_params=_cparams(("parallel",)),
        name="mla_up",
    )(cbuf, gq.reshape(1, -1), gkv.reshape(1, -1), wq, wk, wv, *tabs)


def _attn_specs(tq, seq, dq, dk, dv, q_off, k_off, v_off):
    return [pl.BlockSpec((1, tq, dq), lambda b, h, i: (b, i, q_off + h)),
            pl.BlockSpec((1, seq, dk), lambda b, h, i: (b, 0, k_off + h)),
            pl.BlockSpec((1, seq, dv), lambda b, h, i: (b, 0, v_off + h))]


def _sb_kernel(q_ref, k_ref, v_ref, tri_ref, o_ref, *, t, scale):
    i = pl.program_id(2)
    q = q_ref[0]
    tri = tri_ref[...]
    row = lax.broadcasted_iota(jnp.int32, (t, t), 0)
    col = lax.broadcasted_iota(jnp.int32, (t, t), 1)
    past = col < row

    def block(j, c, acc, keep):
        start = pl.multiple_of(j * t, t)
        ks = k_ref[0, pl.ds(start, t), :]
        vs = v_ref[0, pl.ds(start, t), :]
        z = _dot_nt(q, ks) * scale
        log_beta = jnp.minimum(z, 0.0) - jnp.log1p(jnp.exp(-jnp.abs(z)))
        log_1m = log_beta - z
        if keep is not None:
            log_1m = jnp.where(keep, log_1m, 0.0)
        hi = log_1m.astype(BF16)
        lo = (log_1m - hi.astype(F32)).astype(BF16)
        between = _dot(hi, tri) + _dot(lo, tri) + c
        w = jnp.exp(log_beta + between)
        if keep is not None:
            w = jnp.where(keep, w, 0.0)
        acc = acc + _dot(w.astype(BF16), vs)
        c = c + jnp.sum(log_1m, axis=-1, keepdims=True)
        return c, acc

    c0 = jnp.zeros((t, 1), F32)
    acc0 = jnp.zeros((t, v_ref.shape[2]), F32)
    c, acc = block(i, c0, acc0, past)
    c, acc = block(jnp.maximum(i - 1, 0), c, acc, row < jnp.where(i > 0, t, 0))

    def cond(carry):
        jj, c_max, _, _ = carry
        return jnp.logical_and(jj < i, c_max > SB_EXP_UNDERFLOW)

    def body(carry):
        jj, _, c, acc = carry
        c, acc = block(i - 1 - jj, c, acc, None)
        return jj + 1, jnp.max(c), c, acc

    _, _, c, acc = lax.while_loop(cond, body, (jnp.int32(1), jnp.max(c), c, acc))
    o_ref[0] = acc.astype(o_ref.dtype)


def _sb_attention(qkv, batch, seq, heads, q_off, k_off, v_off):
    t = _pick(seq, (256, 128))
    arr = qkv.reshape(batch, seq, -1)
    tri = (lax.broadcasted_iota(jnp.int32, (t, t), 0) > lax.broadcasted_iota(jnp.int32, (t, t), 1)).astype(BF16)
    specs = _attn_specs(t, seq, HEAD_DIM, HEAD_DIM, HEAD_DIM, q_off, k_off, v_off)
    specs.append(pl.BlockSpec((t, t), lambda b, h, i: (0, 0)))
    return pl.pallas_call(
        functools.partial(_sb_kernel, t=t, scale=HEAD_DIM ** -0.5),
        grid=(batch, heads, seq // t),
        in_specs=specs,
        out_specs=pl.BlockSpec((1, t, HEAD_DIM), lambda b, h, i: (b, i, h)),
        out_shape=jax.ShapeDtypeStruct((batch, seq, heads * HEAD_DIM), BF16),
        compiler_params=_cparams(("parallel", "parallel", "arbitrary")),
        name="sb_attn",
    )(arr, arr, arr, tri)


def _softmax_pv(s, vs, m, l, acc, mask):
    if mask is not None:
        s = jnp.where(mask, s, -jnp.inf)
    m_new = jnp.maximum(m, jnp.max(s, axis=-1, keepdims=True))
    a = jnp.exp2(m - m_new)
    p = jnp.exp2(s - m_new)
    l = a * l + jnp.sum(p, axis=-1, keepdims=True)
    acc = a * acc + _dot(p.astype(BF16), vs)
    return m_new, l, acc


def _flash_init(t, dv):
    return jnp.full((t, 1), -jnp.inf, F32), jnp.zeros((t, 1), F32), jnp.zeros((t, dv), F32)


def _causal_sweep(i, tq, tk, step, init):
    nsub = tq // tk
    row = lax.broadcasted_iota(jnp.int32, (tq, tk), 0)
    col = lax.broadcasted_iota(jnp.int32, (tq, tk), 1)
    def body(jj, c):
        for u in range(nsub):
            c = step(jj * nsub + u, c, None)
        return c

    carry = lax.fori_loop(0, i, body, init)
    for u in range(nsub):
        carry = step(i * nsub + u, carry, col + u * tk <= row)
    return carry


def _flash_tiles(seq):
    tq = _pick(seq, (FLASH_TQ, 512, 256, 128))
    return tq, _pick(tq, (FLASH_TK, 256, 128))


def _mla_kernel(q_ref, k_ref, v_ref, o_ref, *, tq, tk, scale):
    q = q_ref[0]

    def step(j, carry, mask):
        rows = pl.ds(pl.multiple_of(j * tk, tk), tk)
        s = _dot_nt(q, k_ref[0, rows, :]) * scale
        return _softmax_pv(s, v_ref[0, rows, :], *carry, mask)

    m, l, acc = _causal_sweep(pl.program_id(2), tq, tk, step, _flash_init(tq, v_ref.shape[2]))
    o_ref[0] = (acc / l).astype(o_ref.dtype)


def _mla_attention(q, k, v, batch, seq, heads):
    t, tk = _flash_tiles(seq)
    return pl.pallas_call(
        functools.partial(_mla_kernel, tq=t, tk=tk, scale=(MLA_NOPE + MLA_ROPE) ** -0.5 * LOG2E),
        grid=(batch, heads, seq // t),
        in_specs=_attn_specs(t, seq, MLA_QK_PAD, MLA_QK_PAD, MLA_V, 0, 0, 0),
        out_specs=pl.BlockSpec((1, t, MLA_V), lambda b, h, i: (b, i, h)),
        out_shape=jax.ShapeDtypeStruct((batch, seq, heads * MLA_V), BF16),
        compiler_params=_cparams(("parallel", "parallel", "arbitrary")),
        name="mla_attn",
    )(q.reshape(batch, seq, -1), k.reshape(batch, seq, -1), v.reshape(batch, seq, -1))


def _diff_kernel(q_ref, k_ref, v_ref, lam_ref, g_ref, o_ref, *, tq, tk, scale, lam_init):
    d = HEAD_DIM
    q1 = q_ref[0, :, :d]
    q2 = q_ref[0, :, d:]

    def step(j, carry, mask):
        rows = pl.ds(pl.multiple_of(j * tk, tk), tk)
        vs = v_ref[0, rows, :]
        s1 = _dot_nt(q1, k_ref[0, rows, :d]) * scale
        s2 = _dot_nt(q2, k_ref[0, rows, d:]) * scale
        return _softmax_pv(s1, vs, *carry[:3], mask) + _softmax_pv(s2, vs, *carry[3:], mask)

    one = _flash_init(tq, v_ref.shape[2])
    m1, l1, a1, m2, l2, a2 = _causal_sweep(pl.program_id(2), tq, tk, step, one + one)
    lp = lam_ref[...]
    lam = (jnp.exp(jnp.sum(lp[0:1] * lp[1:2], axis=-1, keepdims=True))
           - jnp.exp(jnp.sum(lp[2:3] * lp[3:4], axis=-1, keepdims=True)) + lam_init)
    o = a1 / l1 - lam * (a2 / l2)
    o = o * lax.rsqrt(jnp.mean(o * o, axis=-1, keepdims=True) + RMS_EPS) * g_ref[...]
    o_ref[0] = (o * (1.0 - lam_init)).astype(o_ref.dtype)


def _diff_attention(qk, q_off, k_off, vbuf, v_off, lam_params, subln_g, lam_init, batch, seq, heads):
    t, tk = _flash_tiles(seq)
    dd = 2 * HEAD_DIM
    specs = [pl.BlockSpec((1, t, dd), lambda b, h, i: (b, i, q_off + h)),
             pl.BlockSpec((1, seq, dd), lambda b, h, i: (b, 0, k_off + h)),
             pl.BlockSpec((1, seq, dd), lambda b, h, i: (b, 0, v_off + h)),
             pl.BlockSpec(lam_params.shape, lambda b, h, i: (0, 0)),
             pl.BlockSpec((1, dd), lambda b, h, i: (0, 0))]
    return pl.pallas_call(
        functools.partial(_diff_kernel, tq=t, tk=tk, scale=HEAD_DIM ** -0.5 * LOG2E, lam_init=lam_init),
        grid=(batch, heads, seq // t),
        in_specs=specs,
        out_specs=pl.BlockSpec((1, t, dd), lambda b, h, i: (b, i, h)),
        out_shape=jax.ShapeDtypeStruct((batch, seq, heads * dd), BF16),
        compiler_params=_cparams(("parallel", "parallel", "arbitrary")),
        name="diff_attn",
    )(qk.reshape(batch, seq, -1), qk.reshape(batch, seq, -1), vbuf.reshape(batch, seq, -1),
      lam_params, subln_g.reshape(1, dd))


def _dil_kernel(q_ref, kp_ref, kc_ref, vp_ref, vc_ref, o_ref, acc_scr, m_scr, l_scr, *, scale):
    has_prev = pl.program_id(2) > 0
    ri = lax.broadcasted_iota(jnp.int32, (QBLOCK, QBLOCK), 0)
    ci = lax.broadcasted_iota(jnp.int32, (QBLOCK, QBLOCK), 1)
    ri_first = ri + jnp.where(has_prev, 0, QBLOCK)

    def rows(n, r, dil):
        start = n * QBLOCK * dil + r
        return pl.ds(start, QBLOCK) if dil == 1 else pl.ds(start, QBLOCK, stride=dil)

    def bqk(a, b):
        return lax.dot_general(a, b, (((2,), (2,)), ((0,), (0,))), preferred_element_type=F32)

    def bpv(a, b):
        return lax.dot_general(a, b, (((2,), (1,)), ((0,), (0,))), preferred_element_type=F32)

    for bi, (window, dil) in enumerate(DIL_BRANCHES):
        assert window // dil == QBLOCK
        nblk = DIL_CHUNK // (QBLOCK * dil)
        units = [(n, r) for r in range(dil) for n in range(nblk)]
        loaded = {}
        for g in range(0, len(units), DIL_UNITS):
            group = units[g:g + DIL_UNITS]
            qs, kcs, vcs, kps, vps, thr = [], [], [], [], [], []
            for n, r in group:
                cur = rows(n, r, dil)
                qs.append(q_ref[0, cur, :].astype(BF16))
                loaded[n, r] = (kc_ref[0, cur, :].astype(BF16), vc_ref[0, cur, :].astype(BF16))
                kcs.append(loaded[n, r][0])
                vcs.append(loaded[n, r][1])
                if n > 0:
                    kp, vp = loaded.pop((n - 1, r))
                    thr.append(ri)
                else:
                    prv = rows(nblk - 1, r, dil)
                    kp, vp = kp_ref[0, prv, :].astype(BF16), vp_ref[0, prv, :].astype(BF16)
                    thr.append(ri_first)
                kps.append(kp)
                vps.append(vp)
            q = jnp.stack(qs)
            ci3 = jnp.broadcast_to(ci, (len(group), QBLOCK, QBLOCK))
            s_p = jnp.where(ci3 >= jnp.stack(thr), bqk(q, jnp.stack(kps)) * scale, -jnp.inf)
            s_c = jnp.where(ci3 <= jnp.broadcast_to(ri, ci3.shape), bqk(q, jnp.stack(kcs)) * scale, -jnp.inf)
            mx = jnp.max(jnp.maximum(s_p, s_c), axis=-1, keepdims=True)
            mx = jnp.broadcast_to(mx, ci3.shape)
            e_p = jnp.exp2(s_p - mx)
            e_c = jnp.exp2(s_c - mx)
            lsum = jnp.broadcast_to(jnp.sum(e_p + e_c, axis=-1, keepdims=True), ci3.shape)
            pv = bpv(e_p.astype(BF16), jnp.stack(vps)) + bpv(e_c.astype(BF16), jnp.stack(vcs))
            for u, (n, r) in enumerate(group):
                cur = rows(n, r, dil)
                acc_scr[bi, cur, :] = pv[u]
                m_scr[bi, cur, :] = mx[u]
                l_scr[bi, cur, :] = lsum[u]

    nb = len(DIL_BRANCHES)
    for c in range(DIL_CHUNK // DIL_MERGE_ROWS):
        rs = pl.ds(c * DIL_MERGE_ROWS, DIL_MERGE_ROWS)
        ms = [m_scr[b, rs, :] for b in range(nb)]
        top = functools.reduce(jnp.maximum, ms)
        ws = [jnp.exp2(m - top) for m in ms]
        num = sum(w * acc_scr[b, rs, :] for b, w in enumerate(ws))
        den = sum(w * l_scr[b, rs, :] for b, w in enumerate(ws))
        o_ref[0, rs, :] = (num / den).astype(o_ref.dtype)


def _dil_attention(qk, q_off, k_off, vbuf, v_off, batch, seq, heads):
    assert seq % DIL_CHUNK == 0
    qk3 = qk.reshape(batch, seq, -1)
    v3 = vbuf.reshape(batch, seq, -1)
    blk = (1, DIL_CHUNK, HEAD_DIM)
    cur = lambda off: pl.BlockSpec(blk, lambda b, h, c: (b, c, off + h))
    prev = lambda off: pl.BlockSpec(blk, lambda b, h, c: (b, jnp.maximum(c - 1, 0), off + h))
    return pl.pallas_call(
        functools.partial(_dil_kernel, scale=HEAD_DIM ** -0.5 * LOG2E),
        grid=(batch, heads, seq // DIL_CHUNK),
        in_specs=[cur(q_off), prev(k_off), cur(k_off), prev(v_off), cur(v_off)],
        out_specs=pl.BlockSpec(blk, lambda b, h, c: (b, c, h)),
        out_shape=jax.ShapeDtypeStruct((batch, seq, heads * HEAD_DIM), BF16),
        scratch_shapes=[pltpu.VMEM((len(DIL_BRANCHES), DIL_CHUNK, LANE), F32)] * 3,
        compiler_params=_cparams(("parallel", "parallel", "arbitrary")),
        name="dil_attn",
    )(qk3, qk3, qk3, v3, v3)


def _rope_tables(seq):
    pos = jnp.arange(seq, dtype=jnp.int32).astype(F32)

    def cs(width):
        half = width // 2
        inv = ROPE_THETA ** (-jnp.arange(half, dtype=F32) / half)
        ang = pos[:, None] * inv[None, :]
        return jnp.cos(ang), jnp.sin(ang)

    cos, sin = cs(HEAD_DIM)
    full = (jnp.concatenate([cos, cos], axis=-1), jnp.concatenate([-sin, sin], axis=-1))
    cos, sin = cs(MLA_ROPE)
    z = jnp.zeros_like(cos)
    pad = jnp.zeros((seq, LANE - MLA_ROPE), F32)
    mla = (jnp.concatenate([cos, cos, pad], axis=-1),
           jnp.concatenate([-sin, z, pad], axis=-1),
           jnp.concatenate([z, sin, pad], axis=-1))
    return full, mla


def _layer(xf, xb, p_i, lam_init, alpha, tabs, dims, w_in, w_o, mla_q_norm, mla_w_uq, mla_kv_norm, mla_w_ukv,
           diff_lambda, diff_subln, ln_attn_g, ln_attn_b, w_ff1, w_ff2, ln_ff_g, ln_ff_b,
           w_ple_gate, w_ple_proj, ln_ple_g, ln_ple_b, last):
    batch, seq, d_model = dims
    rope_full, rope_mla = tabs
    mix_heads = d_model // HEAD_DIM
    sb_heads = dil_heads = mla_heads = mix_heads // 4
    diff_heads = mix_heads // 8
    q_rank, kv_rank = mla_q_norm.shape[0], mla_kv_norm.shape[0]
    hw = sb_heads * HEAD_DIM
    widths = (hw,) * 3 + (hw,) * 3 + (q_rank, kv_rank, MLA_ROPE) + (diff_heads * 2 * HEAD_DIM,) * 3
    offs = [0]
    for wd in widths:
        offs.append(offs[-1] + wd)
    col = lambda n: w_in[:, offs[n]:offs[n + 1]]
    a_q, a_k, a_v, b_q, b_k, b_v, c_q, c_kv, c_kr, d_q, d_k, d_v = (col(n) for n in range(12))

    w_plain = jnp.concatenate([a_q, a_k, a_v, d_v], axis=1).astype(BF16)
    w_rope_f = jnp.concatenate([b_q, b_k], axis=1).astype(BF16)
    w_rope_h = jnp.concatenate([d_q, d_k], axis=1).astype(BF16)
    c_used = q_rank + kv_rank + LANE
    c_width = -(-c_used // 1024) * 1024
    w_c = jnp.concatenate([c_q, c_kv, c_kr, jnp.zeros((d_model, c_width - c_used + LANE - MLA_ROPE), w_in.dtype),
                           b_v], axis=1).astype(BF16)

    h_plain = _mm(xb, w_plain, BF16)
    h_rope_f = _mm(xb, w_rope_f, F32, rope=rope_full, seq=seq)
    h_rope_h = _mm(xb, w_rope_h, BF16, rope=rope_full, seq=seq)
    h_c = _mm(xb, w_c, F32)

    y_a = _sb_attention(h_plain, batch, seq, sb_heads, 0, sb_heads, 2 * sb_heads)
    y_b = _dil_attention(h_rope_f, 0, dil_heads, h_c, c_width // HEAD_DIM, batch, seq, dil_heads)

    wq = mla_w_uq.reshape(q_rank, mla_heads, MLA_NOPE + MLA_ROPE)
    wq = jnp.pad(wq, ((0, 0), (0, 0), (0, MLA_QK_PAD - MLA_NOPE - MLA_ROPE))).reshape(q_rank, -1).astype(BF16)
    wkv = mla_w_ukv.reshape(kv_rank, mla_heads, MLA_NOPE + MLA_V)
    wk = wkv[:, :, :MLA_NOPE].reshape(kv_rank, -1).astype(BF16)
    wv = wkv[:, :, MLA_NOPE:].reshape(kv_rank, -1).astype(BF16)
    mq, mk, mv = _mla_up(h_c, c_width, mla_q_norm, mla_kv_norm, wq, wk, wv, rope_mla, seq, mla_heads)
    y_c = _mla_attention(mq, mk, mv, batch, seq, mla_heads)

    y_d = _diff_attention(h_rope_h, 0, diff_heads, h_plain, 3 * sb_heads // 2, diff_lambda, diff_subln,
                          lam_init, batch, seq, diff_heads)

    mix = jnp.concatenate([y_a, y_b, y_c, y_d], axis=-1).reshape(batch * seq, -1)
    xf, xb = _mm_ln(mix, w_o.astype(BF16), xf, ln_attn_g, ln_attn_b, alpha)

    u = _mm_ws_relu2(xb, w_ff1)
    xf, xb = _mm_ln(u, w_ff2.astype(BF16), xf, ln_ff_g, ln_ff_b, alpha)

    xf, xb = _mm_ln(xb, w_ple_gate.astype(BF16), xf, ln_ple_g, ln_ple_b, alpha,
                    ple=(p_i.astype(BF16), w_ple_proj.astype(BF16)), emit_bf16=not last)
    return xf, xb


def kernel(x, p, w_in, w_o, mla_q_norm, mla_w_uq, mla_kv_norm, mla_w_ukv, diff_lambda, diff_subln,
           ln_attn_g, ln_attn_b, w_ff1, w_ff2, ln_ff_g, ln_ff_b, w_ple_gate, w_ple_proj, ln_ple_g, ln_ple_b):
    batch, seq, d_model = x.shape
    depth = w_in.shape[0]
    alpha = (2 * depth) ** 0.25
    tabs = _rope_tables(seq)
    xf = x.reshape(batch * seq, d_model)
    xb = xf.astype(BF16)
    for i in range(depth):
        lam_init = 0.8 - 0.6 * math.exp(-0.3 * i)
        xf, xb = _layer(xf, xb, p[i].reshape(batch * seq, -1), lam_init, alpha, tabs, (batch, seq, d_model),
                        w_in[i], w_o[i], mla_q_norm[i], mla_w_uq[i], mla_kv_norm[i], mla_w_ukv[i],
                        diff_lambda[i], diff_subln[i], ln_attn_g[i], ln_attn_b[i], w_ff1[i], w_ff2[i],
                        ln_ff_g[i], ln_ff_b[i], w_ple_gate[i], w_ple_proj[i], ln_ple_g[i], ln_ple_b[i],
                        last=i == depth - 1)
    return xf.reshape(batch, seq, d_model)
```

```python
import functools
import math

import jax
import jax.numpy as jnp
from jax import lax
from jax.experimental import pallas as pl
from jax.experimental.pallas import tpu as pltpu

F32 = jnp.float32
BF16 = jnp.bfloat16

LANE = 128
HEAD_DIM = 128
QBLOCK = 128
ROPE_THETA = 10000.0
LN_EPS = 1e-5
RMS_EPS = 1e-6
MLA_NOPE = 128
MLA_ROPE = 64
MLA_V = 128
MLA_QK_PAD = 256
DIL_BRANCHES = ((128, 1), (512, 4), (2048, 16))
DIL_CHUNK = QBLOCK * 16
VMEM_LIMIT_BYTES = 56 * 1024 * 1024
SB_EXP_UNDERFLOW = -104.0
LOG2E = 1.4426950408889634
FLASH_TQ = 1024
FLASH_TK = 512
DIL_MERGE_ROWS = 256
DIL_UNITS = 16


def _pick(n, cands):
    for c in cands:
        if n % c == 0:
            return c
    return n


def _cparams(sem):
    return pltpu.CompilerParams(dimension_semantics=sem, vmem_limit_bytes=VMEM_LIMIT_BYTES)


def _dot(a, b):
    return jnp.dot(a, b, preferred_element_type=F32)


def _dot_nt(a, b):
    return lax.dot_general(a, b, (((1,), (1,)), ((), ())), preferred_element_type=F32)


def _mm_kernel(*refs, rope):
    if rope:
        x_ref, w_ref, cos_ref, sin_ref, o_ref = refs
    else:
        x_ref, w_ref, o_ref = refs
    acc = _dot(x_ref[...], w_ref[...])
    if rope:
        cos = cos_ref[...]
        sin = sin_ref[...]
        for c in range(acc.shape[1] // LANE):
            blk = acc[:, c * LANE:(c + 1) * LANE]
            rot = pltpu.roll(blk, LANE // 2, 1)
            o_ref[:, c * LANE:(c + 1) * LANE] = (blk * cos + rot * sin).astype(o_ref.dtype)
    else:
        o_ref[...] = acc.astype(o_ref.dtype)


def _mm(x, w, out_dtype, *, rope=None, seq=None):
    M, K = x.shape
    N = w.shape[1]
    tm = _pick(M, (1024, 512, 256, 128))
    tn = _pick(N, (1024, 512, 256, 128))
    in_specs = [pl.BlockSpec((tm, K), lambda i, j: (i, 0)),
                pl.BlockSpec((K, tn), lambda i, j: (0, j))]
    args = [x, w]
    if rope is not None:
        tm = _pick(seq, (tm, 512, 256, 128))
        in_specs[0] = pl.BlockSpec((tm, K), lambda i, j: (i, 0))
        nsb = seq // tm
        tab = pl.BlockSpec((tm, LANE), lambda i, j: (i % nsb, 0))
        in_specs += [tab, tab]
        args += list(rope)
    return pl.pallas_call(
        functools.partial(_mm_kernel, rope=rope is not None),
        grid=(M // tm, N // tn),
        in_specs=in_specs,
        out_specs=pl.BlockSpec((tm, tn), lambda i, j: (i, j)),
        out_shape=jax.ShapeDtypeStruct((M, N), out_dtype),
        compiler_params=_cparams(("parallel", "parallel")),
        name="proj_mm",
    )(*args)


WS_CAST_ROWS = 512


def _mm_ws_kernel(x_ref, w_ref, o_ref, wb_ref):
    @pl.when(pl.program_id(1) == 0)
    def _():
        for r in range(0, w_ref.shape[0], WS_CAST_ROWS):
            wb_ref[r:r + WS_CAST_ROWS, :] = w_ref[r:r + WS_CAST_ROWS, :].astype(BF16)

    acc = _dot(x_ref[...], wb_ref[...])
    o_ref[...] = jnp.square(jnp.maximum(acc, 0.0)).astype(o_ref.dtype)


def _mm_ws_relu2(x, w_stack, layer):
    M, K = x.shape
    N = w_stack.shape[2]
    tm = _pick(M, (512, 256, 128))
    tn = _pick(N, (1024, 512, 256, 128))
    assert K % WS_CAST_ROWS == 0
    return pl.pallas_call(
        _mm_ws_kernel,
        grid=(N // tn, M // tm),
        in_specs=[pl.BlockSpec((tm, K), lambda j, i: (i, 0)),
                  pl.BlockSpec((None, K, tn), lambda j, i: (layer, 0, j))],
        out_specs=pl.BlockSpec((tm, tn), lambda j, i: (i, j)),
        out_shape=jax.ShapeDtypeStruct((M, N), BF16),
        scratch_shapes=[pltpu.VMEM((K, tn), BF16)],
        compiler_params=_cparams(("arbitrary", "arbitrary")),
        name="ff1_mm",
    )(x, w_stack)


LN_ROWS = 32
LN_CHAINS = 2
LN_COLS = 1024
LN_STEP_ROWS = 128


def _mm_ln_kernel(*refs, nk, alpha, ple, emit_bf16):
    refs = list(refs)
    x_ref, w_ref, r_ref, g_ref, b_ref = refs[:5]
    rest = refs[5:]
    if ple:
        p_ref, wp_ref = rest[:2]
        rest = rest[2:]
    o_ref = rest[0]
    o16_ref = rest[1] if emit_bf16 else None
    acc_ref = rest[-1]
    k = pl.program_id(1)
    n_total = acc_ref.shape[1]
    nc = _pick(n_total, (LN_COLS, LANE))
    rc = o_ref.shape[0]

    @pl.when(k == 0)
    def _():
        acc_ref[...] = jnp.zeros_like(acc_ref)

    @pl.when(k < nk)
    def _():
        x = x_ref[...]
        for c in range(n_total // nc):
            cols = slice(c * nc, (c + 1) * nc)
            acc_ref[:, cols] += _dot(x, w_ref[:, cols])

    @pl.when(k >= nk)
    def _():
        g = g_ref[...]
        b = b_ref[...]
        base = pl.multiple_of((k - nk) * rc, rc)

        def norm_rows(lo):
            f = acc_ref[pl.ds(base + lo, LN_ROWS), :]
            if ple:
                f = (0.5 * jnp.tanh(0.5 * f) + 0.5) * _dot(p_ref[lo:lo + LN_ROWS, :], wp_ref[...])
            y = alpha * r_ref[lo:lo + LN_ROWS, :] + f
            mu = jnp.mean(y, axis=-1, keepdims=True)
            d = y - mu
            var = jnp.mean(d * d, axis=-1, keepdims=True)
            return d * lax.rsqrt(var + LN_EPS) * g + b

        for lo0 in range(0, rc, LN_ROWS * LN_CHAINS):
            los = [lo0 + u * LN_ROWS for u in range(LN_CHAINS) if lo0 + u * LN_ROWS < rc]
            outs = [norm_rows(lo) for lo in los]
            for lo, out in zip(los, outs):
                o_ref[lo:lo + LN_ROWS, :] = out
                if emit_bf16:
                    o16_ref[lo:lo + LN_ROWS, :] = out.astype(BF16)


def _mm_ln(x, w, resid, g, b, alpha, *, ple=None, emit_bf16=True):
    M, K = x.shape
    N = w.shape[1]
    tm = _pick(M, (1024, 512, 256, 128))
    tk = _pick(K, (1024, 512, 256, 128))
    rc = _pick(tm, (LN_STEP_ROWS, LN_ROWS))
    nk, ne = K // tk, tm // rc
    kk = lambda k: jnp.minimum(k, nk - 1)
    chunk = lambda i, k: (i * ne + jnp.maximum(k - nk, 0), 0)
    const = lambda i, k: (0, 0)
    in_specs = [pl.BlockSpec((tm, tk), lambda i, k: (i, kk(k))),
                pl.BlockSpec((tk, N), lambda i, k: (kk(k), 0)),
                pl.BlockSpec((rc, N), chunk),
                pl.BlockSpec((1, N), const),
                pl.BlockSpec((1, N), const)]
    args = [x, w, resid, g.reshape(1, N), b.reshape(1, N)]
    if ple is not None:
        pp, wp = ple
        in_specs += [pl.BlockSpec((rc, pp.shape[1]), chunk), pl.BlockSpec(wp.shape, const)]
        args += [pp, wp]
    out_specs = [pl.BlockSpec((rc, N), chunk)]
    out_shape = [jax.ShapeDtypeStruct((M, N), F32)]
    if emit_bf16:
        out_specs.append(pl.BlockSpec((rc, N), chunk))
        out_shape.append(jax.ShapeDtypeStruct((M, N), BF16))
    outs = pl.pallas_call(
        functools.partial(_mm_ln_kernel, nk=nk, alpha=alpha, ple=ple is not None, emit_bf16=emit_bf16),
        grid=(M // tm, nk + ne),
        in_specs=in_specs,
        out_specs=out_specs,
        out_shape=out_shape,
        scratch_shapes=[pltpu.VMEM((tm, N), F32)],
        compiler_params=_cparams(("parallel", "arbitrary")),
        name="mm_res_ln",
    )(*args)
    return (outs[0], outs[1]) if emit_bf16 else (outs[0], None)


def _rope64(xr, cos, s_lo, s_hi):
    return xr * cos + pltpu.roll(xr, LANE - MLA_ROPE // 2, 1) * s_lo + pltpu.roll(xr, MLA_ROPE // 2, 1) * s_hi


def _mla_up_kernel(c_ref, gq_ref, gkv_ref, wq_ref, wk_ref, wv_ref, cos_ref, slo_ref, shi_ref,
                   q_ref, k_ref, v_ref, *, q_rank, kv_rank, heads):
    cos = cos_ref[...]
    s_lo = slo_ref[...]
    s_hi = shi_ref[...]

    def rms(t, g):
        return (t * lax.rsqrt(jnp.mean(t * t, axis=-1, keepdims=True) + RMS_EPS) * g).astype(BF16)

    nq = rms(c_ref[:, :q_rank], gq_ref[...])
    qf = _dot(nq, wq_ref[...])
    for h in range(heads):
        lo = h * MLA_QK_PAD
        q_ref[:, lo:lo + MLA_NOPE] = qf[:, lo:lo + MLA_NOPE].astype(BF16)
        q_ref[:, lo + MLA_NOPE:lo + MLA_QK_PAD] = _rope64(
            qf[:, lo + MLA_NOPE:lo + MLA_QK_PAD], cos, s_lo, s_hi).astype(BF16)
    nkv = rms(c_ref[:, q_rank:q_rank + kv_rank], gkv_ref[...])
    kf = _dot(nkv, wk_ref[...])
    v_ref[...] = _dot(nkv, wv_ref[...]).astype(BF16)
    kr = _rope64(c_ref[:, q_rank + kv_rank:q_rank + kv_rank + LANE], cos, s_lo, s_hi).astype(BF16)
    for h in range(heads):
        lo = h * MLA_QK_PAD
        k_ref[:, lo:lo + MLA_NOPE] = kf[:, h * MLA_NOPE:(h + 1) * MLA_NOPE].astype(BF16)
        k_ref[:, lo + MLA_NOPE:lo + MLA_QK_PAD] = kr


def _mla_up(cbuf, c_width, gq, gkv, wq, wk, wv, tabs, seq, heads):
    M = cbuf.shape[0]
    q_rank, kv_rank = gq.shape[0], gkv.shape[0]
    tm = _pick(seq, (512, 256, 128))
    nsb = seq // tm
    const = lambda shape: pl.BlockSpec(shape, lambda i: (0, 0))
    tab = pl.BlockSpec((tm, LANE), lambda i: (i % nsb, 0))
    row = lambda n: pl.BlockSpec((tm, n), lambda i: (i, 0))
    return pl.pallas_call(
        functools.partial(_mla_up_kernel, q_rank=q_rank, kv_rank=kv_rank, heads=heads),
        grid=(M // tm,),
        in_specs=[row(c_width), const((1, q_rank)), const((1, kv_rank)),
                  const(wq.shape), const(wk.shape), const(wv.shape), tab, tab, tab],
        out_specs=[row(heads * MLA_QK_PAD), row(heads * MLA_QK_PAD), row(heads * MLA_V)],
        out_shape=[jax.ShapeDtypeStruct((M, heads * MLA_QK_PAD), BF16),
                   jax.ShapeDtypeStruct((M, heads * MLA_QK_PAD), BF16),
                   jax.ShapeDtypeStruct((M, heads * MLA_V), BF16)],
        compiler_params=_cparams(("parallel",)),
        name="mla_up",
    )(cbuf, gq.reshape(1, -1), gkv.reshape(1, -1), wq, wk, wv, *tabs)


def _attn_specs(tq, seq, dq, dk, dv, q_off, k_off, v_off):
    return [pl.BlockSpec((1, tq, dq), lambda b, h, i: (b, i, q_off + h)),
            pl.BlockSpec((1, seq, dk), lambda b, h, i: (b, 0, k_off + h)),
            pl.BlockSpec((1, seq, dv), lambda b, h, i: (b, 0, v_off + h))]


def _sb_kernel(q_ref, k_ref, v_ref, tri_ref, o_ref, *, t, scale):
    i = pl.program_id(2)
    q = q_ref[0]
    tri = tri_ref[...]
    row = lax.broadcasted_iota(jnp.int32, (t, t), 0)
    col = lax.broadcasted_iota(jnp.int32, (t, t), 1)
    past = col < row

    def block(j, c, acc, keep):
        start = pl.multiple_of(j * t, t)
        ks = k_ref[0, pl.ds(start, t), :]
        vs = v_ref[0, pl.ds(start, t), :]
        z = _dot_nt(q, ks) * scale
        log_beta = jnp.minimum(z, 0.0) - jnp.log1p(jnp.exp(-jnp.abs(z)))
        log_1m = log_beta - z
        if keep is not None:
            log_1m = jnp.where(keep, log_1m, 0.0)
        hi = log_1m.astype(BF16)
        lo = (log_1m - hi.astype(F32)).astype(BF16)
        between = _dot(hi, tri) + _dot(lo, tri) + c
        w = jnp.exp(log_beta + between)
        if keep is not None:
            w = jnp.where(keep, w, 0.0)
        acc = acc + _dot(w.astype(BF16), vs)
        c = c + jnp.sum(log_1m, axis=-1, keepdims=True)
        return c, acc

    c0 = jnp.zeros((t, 1), F32)
    acc0 = jnp.zeros((t, v_ref.shape[2]), F32)
    c, acc = block(i, c0, acc0, past)
    c, acc = block(jnp.maximum(i - 1, 0), c, acc, row < jnp.where(i > 0, t, 0))

    def cond(carry):
        jj, c_max, _, _ = carry
        return jnp.logical_and(jj < i, c_max > SB_EXP_UNDERFLOW)

    def body(carry):
        jj, _, c, acc = carry
        c, acc = block(i - 1 - jj, c, acc, None)
        return jj + 1, jnp.max(c), c, acc

    _, _, c, acc = lax.while_loop(cond, body, (jnp.int32(1), jnp.max(c), c, acc))
    o_ref[0] = acc.astype(o_ref.dtype)


def _sb_attention(qkv, batch, seq, heads, q_off, k_off, v_off):
    t = _pick(seq, (256, 128))
    arr = qkv.reshape(batch, seq, -1)
    tri = (lax.broadcasted_iota(jnp.int32, (t, t), 0) > lax.broadcasted_iota(jnp.int32, (t, t), 1)).astype(BF16)
    specs = _attn_specs(t, seq, HEAD_DIM, HEAD_DIM, HEAD_DIM, q_off, k_off, v_off)
    specs.append(pl.BlockSpec((t, t), lambda b, h, i: (0, 0)))
    return pl.pallas_call(
        functools.partial(_sb_kernel, t=t, scale=HEAD_DIM ** -0.5),
        grid=(batch, heads, seq // t),
        in_specs=specs,
        out_specs=pl.BlockSpec((1, t, HEAD_DIM), lambda b, h, i: (b, i, h)),
        out_shape=jax.ShapeDtypeStruct((batch, seq, heads * HEAD_DIM), BF16),
        compiler_params=_cparams(("parallel", "parallel", "arbitrary")),
        name="sb_attn",
    )(arr, arr, arr, tri)


def _softmax_pv(s, vs, m, l, acc, mask):
    if mask is not None:
        s = jnp.where(mask, s, -jnp.inf)
    m_new = jnp.maximum(m, jnp.max(s, axis=-1, keepdims=True))
    a = jnp.exp2(m - m_new)
    p = jnp.exp2(s - m_new)
    l = a * l + jnp.sum(p, axis=-1, keepdims=True)
    acc = a * acc + _dot(p.astype(BF16), vs)
    return m_new, l, acc


def _flash_init(t, dv):
    return jnp.full((t, 1), -jnp.inf, F32), jnp.zeros((t, 1), F32), jnp.zeros((t, dv), F32)


def _causal_sweep(i, tq, tk, step, init):
    nsub = tq // tk
    row = lax.broadcasted_iota(jnp.int32, (tq, tk), 0)
    col = lax.broadcasted_iota(jnp.int32, (tq, tk), 1)
    def body(jj, c):
        for u in range(nsub):
            c = step(jj * nsub + u, c, None)
        return c

    carry = lax.fori_loop(0, i, body, init)
    for u in range(nsub):
        carry = step(i * nsub + u, carry, col + u * tk <= row)
    return carry


def _flash_tiles(seq):
    tq = _pick(seq, (FLASH_TQ, 512, 256, 128))
    return tq, _pick(tq, (FLASH_TK, 256, 128))


def _mla_kernel(q_ref, k_ref, v_ref, o_ref, *, tq, tk, scale):
    q = q_ref[0]

    def step(j, carry, mask):
        rows = pl.ds(pl.multiple_of(j * tk, tk), tk)
        s = _dot_nt(q, k_ref[0, rows, :]) * scale
        return _softmax_pv(s, v_ref[0, rows, :], *carry, mask)

    m, l, acc = _causal_sweep(pl.program_id(2), tq, tk, step, _flash_init(tq, v_ref.shape[2]))
    o_ref[0] = (acc / l).astype(o_ref.dtype)


def _mla_attention(q, k, v, batch, seq, heads):
    t, tk = _flash_tiles(seq)
    return pl.pallas_call(
        functools.partial(_mla_kernel, tq=t, tk=tk, scale=(MLA_NOPE + MLA_ROPE) ** -0.5 * LOG2E),
        grid=(batch, heads, seq // t),
        in_specs=_attn_specs(t, seq, MLA_QK_PAD, MLA_QK_PAD, MLA_V, 0, 0, 0),
        out_specs=pl.BlockSpec((1, t, MLA_V), lambda b, h, i: (b, i, h)),
        out_shape=jax.ShapeDtypeStruct((batch, seq, heads * MLA_V), BF16),
        compiler_params=_cparams(("parallel", "parallel", "arbitrary")),
        name="mla_attn",
    )(q.reshape(batch, seq, -1), k.reshape(batch, seq, -1), v.reshape(batch, seq, -1))


def _diff_kernel(q_ref, k_ref, v_ref, lam_ref, g_ref, o_ref, *, tq, tk, scale, lam_init):
    d = HEAD_DIM
    q1 = q_ref[0, :, :d]
    q2 = q_ref[0, :, d:]

    def step(j, carry, mask):
        rows = pl.ds(pl.multiple_of(j * tk, tk), tk)
        vs = v_ref[0, rows, :]
        s1 = _dot_nt(q1, k_ref[0, rows, :d]) * scale
        s2 = _dot_nt(q2, k_ref[0, rows, d:]) * scale
        return _softmax_pv(s1, vs, *carry[:3], mask) + _softmax_pv(s2, vs, *carry[3:], mask)

    one = _flash_init(tq, v_ref.shape[2])
    m1, l1, a1, m2, l2, a2 = _causal_sweep(pl.program_id(2), tq, tk, step, one + one)
    lp = lam_ref[...]
    lam = (jnp.exp(jnp.sum(lp[0:1] * lp[1:2], axis=-1, keepdims=True))
           - jnp.exp(jnp.sum(lp[2:3] * lp[3:4], axis=-1, keepdims=True)) + lam_init)
    o = a1 / l1 - lam * (a2 / l2)
    o = o * lax.rsqrt(jnp.mean(o * o, axis=-1, keepdims=True) + RMS_EPS) * g_ref[...]
    o_ref[0] = (o * (1.0 - lam_init)).astype(o_ref.dtype)


def _diff_attention(qk, q_off, k_off, vbuf, v_off, lam_params, subln_g, lam_init, batch, seq, heads):
    t, tk = _flash_tiles(seq)
    dd = 2 * HEAD_DIM
    specs = [pl.BlockSpec((1, t, dd), lambda b, h, i: (b, i, q_off + h)),
             pl.BlockSpec((1, seq, dd), lambda b, h, i: (b, 0, k_off + h)),
             pl.BlockSpec((1, seq, dd), lambda b, h, i: (b, 0, v_off + h)),
             pl.BlockSpec(lam_params.shape, lambda b, h, i: (0, 0)),
             pl.BlockSpec((1, dd), lambda b, h, i: (0, 0))]
    return pl.pallas_call(
        functools.partial(_diff_kernel, tq=t, tk=tk, scale=HEAD_DIM ** -0.5 * LOG2E, lam_init=lam_init),
        grid=(batch, heads, seq // t),
        in_specs=specs,
        out_specs=pl.BlockSpec((1, t, dd), lambda b, h, i: (b, i, h)),
        out_shape=jax.ShapeDtypeStruct((batch, seq, heads * dd), BF16),
        compiler_params=_cparams(("parallel", "parallel", "arbitrary")),
        name="diff_attn",
    )(qk.reshape(batch, seq, -1), qk.reshape(batch, seq, -1), vbuf.reshape(batch, seq, -1),
      lam_params, subln_g.reshape(1, dd))


def _dil_kernel(q_ref, kp_ref, kc_ref, vp_ref, vc_ref, o_ref, acc_scr, m_scr, l_scr, *, scale):
    has_prev = pl.program_id(2) > 0
    ri = lax.broadcasted_iota(jnp.int32, (QBLOCK, QBLOCK), 0)
    ci = lax.broadcasted_iota(jnp.int32, (QBLOCK, QBLOCK), 1)
    ri_first = ri + jnp.where(has_prev, 0, QBLOCK)

    def rows(n, r, dil):
        start = n * QBLOCK * dil + r
        return pl.ds(start, QBLOCK) if dil == 1 else pl.ds(start, QBLOCK, stride=dil)

    def bqk(a, b):
        return lax.dot_general(a, b, (((2,), (2,)), ((0,), (0,))), preferred_element_type=F32)

    def bpv(a, b):
        return lax.dot_general(a, b, (((2,), (1,)), ((0,), (0,))), preferred_element_type=F32)

    for bi, (window, dil) in enumerate(DIL_BRANCHES):
        assert window // dil == QBLOCK
        nblk = DIL_CHUNK // (QBLOCK * dil)
        units = [(n, r) for r in range(dil) for n in range(nblk)]
        loaded = {}
        for g in range(0, len(units), DIL_UNITS):
            group = units[g:g + DIL_UNITS]
            qs, kcs, vcs, kps, vps, thr = [], [], [], [], [], []
            for n, r in group:
                cur = rows(n, r, dil)
                qs.append(q_ref[0, cur, :].astype(BF16))
                loaded[n, r] = (kc_ref[0, cur, :].astype(BF16), vc_ref[0, cur, :].astype(BF16))
                kcs.append(loaded[n, r][0])
                vcs.append(loaded[n, r][1])
                if n > 0:
                    kp, vp = loaded.pop((n - 1, r))
                    thr.append(ri)
                else:
                    prv = rows(nblk - 1, r, dil)
                    kp, vp = kp_ref[0, prv, :].astype(BF16), vp_ref[0, prv, :].astype(BF16)
                    thr.append(ri_first)
                kps.append(kp)
                vps.append(vp)
            q = jnp.stack(qs)
            ci3 = jnp.broadcast_to(ci, (len(group), QBLOCK, QBLOCK))
            s_p = jnp.where(ci3 >= jnp.stack(thr), bqk(q, jnp.stack(kps)) * scale, -jnp.inf)
            s_c = jnp.where(ci3 <= jnp.broadcast_to(ri, ci3.shape), bqk(q, jnp.stack(kcs)) * scale, -jnp.inf)
            mx = jnp.max(jnp.maximum(s_p, s_c), axis=-1, keepdims=True)
            mx = jnp.broadcast_to(mx, ci3.shape)
            e_p = jnp.exp2(s_p - mx)
            e_c = jnp.exp2(s_c - mx)
            lsum = jnp.broadcast_to(jnp.sum(e_p + e_c, axis=-1, keepdims=True), ci3.shape)
            pv = bpv(e_p.astype(BF16), jnp.stack(vps)) + bpv(e_c.astype(BF16), jnp.stack(vcs))
            for u, (n, r) in enumerate(group):
                cur = rows(n, r, dil)
                acc_scr[bi, cur, :] = pv[u]
                m_scr[bi, cur, :] = mx[u]
                l_scr[bi, cur, :] = lsum[u]

    nb = len(DIL_BRANCHES)
    for c in range(DIL_CHUNK // DIL_MERGE_ROWS):
        rs = pl.ds(c * DIL_MERGE_ROWS, DIL_MERGE_ROWS)
        ms = [m_scr[b, rs, :] for b in range(nb)]
        top = functools.reduce(jnp.maximum, ms)
        ws = [jnp.exp2(m - top) for m in ms]
        num = sum(w * acc_scr[b, rs, :] for b, w in enumerate(ws))
        den = sum(w * l_scr[b, rs, :] for b, w in enumerate(ws))
        o_ref[0, rs, :] = (num / den).astype(o_ref.dtype)


def _dil_attention(qk, q_off, k_off, vbuf, v_off, batch, seq, heads):
    assert seq % DIL_CHUNK == 0
    qk3 = qk.reshape(batch, seq, -1)
    v3 = vbuf.reshape(batch, seq, -1)
    blk = (1, DIL_CHUNK, HEAD_DIM)
    cur = lambda off: pl.BlockSpec(blk, lambda b, h, c: (b, c, off + h))
    prev = lambda off: pl.BlockSpec(blk, lambda b, h, c: (b, jnp.maximum(c - 1, 0), off + h))
    return pl.pallas_call(
        functools.partial(_dil_kernel, scale=HEAD_DIM ** -0.5 * LOG2E),
        grid=(batch, heads, seq // DIL_CHUNK),
        in_specs=[cur(q_off), prev(k_off), cur(k_off), prev(v_off), cur(v_off)],
        out_specs=pl.BlockSpec(blk, lambda b, h, c: (b, c, h)),
        out_shape=jax.ShapeDtypeStruct((batch, seq, heads * HEAD_DIM), BF16),
        scratch_shapes=[pltpu.VMEM((len(DIL_BRANCHES), DIL_CHUNK, LANE), F32)] * 3,
        compiler_params=_cparams(("parallel", "parallel", "arbitrary")),
        name="dil_attn",
    )(qk3, qk3, qk3, v3, v3)


def _rope_tables(seq):
    pos = jnp.arange(seq, dtype=jnp.int32).astype(F32)

    def cs(width):
        half = width // 2
        inv = ROPE_THETA ** (-jnp.arange(half, dtype=F32) / half)
        ang = pos[:, None] * inv[None, :]
        return jnp.cos(ang), jnp.sin(ang)

    cos, sin = cs(HEAD_DIM)
    full = (jnp.concatenate([cos, cos], axis=-1), jnp.concatenate([-sin, sin], axis=-1))
    cos, sin = cs(MLA_ROPE)
    z = jnp.zeros_like(cos)
    pad = jnp.zeros((seq, LANE - MLA_ROPE), F32)
    mla = (jnp.concatenate([cos, cos, pad], axis=-1),
           jnp.concatenate([-sin, z, pad], axis=-1),
           jnp.concatenate([z, sin, pad], axis=-1))
    return full, mla


def _layer(xf, xb, p_i, lam_init, alpha, tabs, dims, w_in, w_o, mla_q_norm, mla_w_uq, mla_kv_norm, mla_w_ukv,
           diff_lambda, diff_subln, ln_attn_g, ln_attn_b, w_ff1, w_ff2, ln_ff_g, ln_ff_b,
           w_ple_gate, w_ple_proj, ln_ple_g, ln_ple_b, last):
    batch, seq, d_model = dims
    rope_full, rope_mla = tabs
    mix_heads = d_model // HEAD_DIM
    sb_heads = dil_heads = mla_heads = mix_heads // 4
    diff_heads = mix_heads // 8
    q_rank, kv_rank = mla_q_norm.shape[0], mla_kv_norm.shape[0]
    hw = sb_heads * HEAD_DIM
    widths = (hw,) * 3 + (hw,) * 3 + (q_rank, kv_rank, MLA_ROPE) + (diff_heads * 2 * HEAD_DIM,) * 3
    offs = [0]
    for wd in widths:
        offs.append(offs[-1] + wd)
    col = lambda n: w_in[:, offs[n]:offs[n + 1]]
    a_q, a_k, a_v, b_q, b_k, b_v, c_q, c_kv, c_kr, d_q, d_k, d_v = (col(n) for n in range(12))

    w_plain = jnp.concatenate([a_q, a_k, a_v, d_v], axis=1).astype(BF16)
    w_rope_f = jnp.concatenate([b_q, b_k], axis=1).astype(BF16)
    w_rope_h = jnp.concatenate([d_q, d_k], axis=1).astype(BF16)
    c_used = q_rank + kv_rank + LANE
    c_width = -(-c_used // 1024) * 1024
    w_c = jnp.concatenate([c_q, c_kv, c_kr, jnp.zeros((d_model, c_width - c_used + LANE - MLA_ROPE), w_in.dtype),
                           b_v], axis=1).astype(BF16)

    h_plain = _mm(xb, w_plain, BF16)
    h_rope_f = _mm(xb, w_rope_f, F32, rope=rope_full, seq=seq)
    h_rope_h = _mm(xb, w_rope_h, BF16, rope=rope_full, seq=seq)
    h_c = _mm(xb, w_c, F32)

    y_a = _sb_attention(h_plain, batch, seq, sb_heads, 0, sb_heads, 2 * sb_heads)
    y_b = _dil_attention(h_rope_f, 0, dil_heads, h_c, c_width // HEAD_DIM, batch, seq, dil_heads)

    wq = mla_w_uq.reshape(q_rank, mla_heads, MLA_NOPE + MLA_ROPE)
    wq = jnp.pad(wq, ((0, 0), (0, 0), (0, MLA_QK_PAD - MLA_NOPE - MLA_ROPE))).reshape(q_rank, -1).astype(BF16)
    wkv = mla_w_ukv.reshape(kv_rank, mla_heads, MLA_NOPE + MLA_V)
    wk = wkv[:, :, :MLA_NOPE].reshape(kv_rank, -1).astype(BF16)
    wv = wkv[:, :, MLA_NOPE:].reshape(kv_rank, -1).astype(BF16)
    mq, mk, mv = _mla_up(h_c, c_width, mla_q_norm, mla_kv_norm, wq, wk, wv, rope_mla, seq, mla_heads)
    y_c = _mla_attention(mq, mk, mv, batch, seq, mla_heads)

    y_d = _diff_attention(h_rope_h, 0, diff_heads, h_plain, 3 * sb_heads // 2, diff_lambda, diff_subln,
                          lam_init, batch, seq, diff_heads)

    mix = jnp.concatenate([y_a, y_b, y_c, y_d], axis=-1).reshape(batch * seq, -1)
    xf, xb = _mm_ln(mix, w_o.astype(BF16), xf, ln_attn_g, ln_attn_b, alpha)

    u = _mm_ws_relu2(xb, *w_ff1)
    xf, xb = _mm_ln(u, w_ff2.astype(BF16), xf, ln_ff_g, ln_ff_b, alpha)

    xf, xb = _mm_ln(xb, w_ple_gate.astype(BF16), xf, ln_ple_g, ln_ple_b, alpha,
                    ple=(p_i.astype(BF16), w_ple_proj.astype(BF16)), emit_bf16=not last)
    return xf, xb


def kernel(x, p, w_in, w_o, mla_q_norm, mla_w_uq, mla_kv_norm, mla_w_ukv, diff_lambda, diff_subln,
           ln_attn_g, ln_attn_b, w_ff1, w_ff2, ln_ff_g, ln_ff_b, w_ple_gate, w_ple_proj, ln_ple_g, ln_ple_b):
    batch, seq, d_model = x.shape
    depth = w_in.shape[0]
    alpha = (2 * depth) ** 0.25
    tabs = _rope_tables(seq)
    xf = x.reshape(batch * seq, d_model)
    xb = xf.astype(BF16)
    for i in range(depth):
        lam_init = 0.8 - 0.6 * math.exp(-0.3 * i)
        xf, xb = _layer(xf, xb, p[i].reshape(batch * seq, -1), lam_init, alpha, tabs, (batch, seq, d_model),
                        w_in[i], w_o[i], mla_q_norm[i], mla_w_uq[i], mla_kv_norm[i], mla_w_ukv[i],
                        diff_lambda[i], diff_subln[i], ln_attn_g[i], ln_attn_b[i], (w_ff1, i), w_ff2[i],
                        ln_ff_g[i], ln_ff_b[i], w_ple_gate[i], w_ple_proj[i], ln_ple_g[i], ln_ple_b[i],
                        last=i == depth - 1)
    return xf.reshape(batch, seq, d_model)
```

```python
import functools
import math

import jax
import jax.numpy as jnp
from jax import lax
from jax.experimental import pallas as pl
from jax.experimental.pallas import tpu as pltpu

F32 = jnp.float32
BF16 = jnp.bfloat16

LANE = 128
HEAD_DIM = 128
QBLOCK = 128
ROPE_THETA = 10000.0
LN_EPS = 1e-5
RMS_EPS = 1e-6
MLA_NOPE = 128
MLA_ROPE = 64
MLA_V = 128
MLA_QK_PAD = 256
DIL_BRANCHES = ((128, 1), (512, 4), (2048, 16))
DIL_CHUNK = QBLOCK * 16
VMEM_LIMIT_BYTES = 56 * 1024 * 1024
SB_EXP_UNDERFLOW = -104.0
LOG2E = 1.4426950408889634
FLASH_TQ = 1024
FLASH_TK = 512
DIL_MERGE_ROWS = 256
DIL_UNITS = 16


def _pick(n, cands):
    for c in cands:
        if n % c == 0:
            return c
    return n


def _cparams(sem):
    return pltpu.CompilerParams(dimension_semantics=sem, vmem_limit_bytes=VMEM_LIMIT_BYTES)


def _dot(a, b):
    return jnp.dot(a, b, preferred_element_type=F32)


def _dot_nt(a, b):
    return lax.dot_general(a, b, (((1,), (1,)), ((), ())), preferred_element_type=F32)


def _mm_kernel(*refs, rope):
    if rope:
        x_ref, w_ref, cos_ref, sin_ref, o_ref = refs
    else:
        x_ref, w_ref, o_ref = refs
    acc = _dot(x_ref[...], w_ref[...])
    if rope:
        cos = cos_ref[...]
        sin = sin_ref[...]
        for c in range(acc.shape[1] // LANE):
            blk = acc[:, c * LANE:(c + 1) * LANE]
            rot = pltpu.roll(blk, LANE // 2, 1)
            o_ref[:, c * LANE:(c + 1) * LANE] = (blk * cos + rot * sin).astype(o_ref.dtype)
    else:
        o_ref[...] = acc.astype(o_ref.dtype)


def _mm(x, w, out_dtype, *, rope=None, seq=None):
    M, K = x.shape
    N = w.shape[1]
    tm = _pick(M, (1024, 512, 256, 128))
    tn = _pick(N, (1024, 512, 256, 128))
    in_specs = [pl.BlockSpec((tm, K), lambda i, j: (i, 0)),
                pl.BlockSpec((K, tn), lambda i, j: (0, j))]
    args = [x, w]
    if rope is not None:
        tm = _pick(seq, (tm, 512, 256, 128))
        in_specs[0] = pl.BlockSpec((tm, K), lambda i, j: (i, 0))
        nsb = seq // tm
        tab = pl.BlockSpec((tm, LANE), lambda i, j: (i % nsb, 0))
        in_specs += [tab, tab]
        args += list(rope)
    return pl.pallas_call(
        functools.partial(_mm_kernel, rope=rope is not None),
        grid=(M // tm, N // tn),
        in_specs=in_specs,
        out_specs=pl.BlockSpec((tm, tn), lambda i, j: (i, j)),
        out_shape=jax.ShapeDtypeStruct((M, N), out_dtype),
        compiler_params=_cparams(("parallel", "parallel")),
        name="proj_mm",
    )(*args)


WS_CAST_ROWS = 512


def _mm_ws_kernel(x_ref, w_ref, o_ref, wb_ref):
    @pl.when(pl.program_id(1) == 0)
    def _():
        for r in range(0, w_ref.shape[0], WS_CAST_ROWS):
            wb_ref[r:r + WS_CAST_ROWS, :] = w_ref[r:r + WS_CAST_ROWS, :].astype(BF16)

    acc = _dot(x_ref[...], wb_ref[...])
    o_ref[...] = jnp.square(jnp.maximum(acc, 0.0)).astype(o_ref.dtype)


def _mm_ws_relu2(x, w_stack, layer):
    M, K = x.shape
    N = w_stack.shape[2]
    tm = _pick(M, (512, 256, 128))
    tn = _pick(N, (1024, 512, 256, 128))
    assert K % WS_CAST_ROWS == 0
    return pl.pallas_call(
        _mm_ws_kernel,
        grid=(N // tn, M // tm),
        in_specs=[pl.BlockSpec((tm, K), lambda j, i: (i, 0)),
                  pl.BlockSpec((None, K, tn), lambda j, i: (layer, 0, j))],
        out_specs=pl.BlockSpec((tm, tn), lambda j, i: (i, j)),
        out_shape=jax.ShapeDtypeStruct((M, N), BF16),
        scratch_shapes=[pltpu.VMEM((K, tn), BF16)],
        compiler_params=_cparams(("arbitrary", "arbitrary")),
        name="ff1_mm",
    )(x, w_stack)


LN_ROWS = 32
LN_CHAINS = 2
LN_COLS = 1024


def _mm_ln_kernel(*refs, n_i, alpha, ple, emit_bf16):
    refs = list(refs)
    x_ref, w_ref, r_ref, g_ref, b_ref = refs[:5]
    rest = refs[5:]
    if ple:
        p_ref, wp_ref = rest[:2]
        rest = rest[2:]
    o_ref = rest[0]
    o16_ref = rest[1] if emit_bf16 else None
    accs = rest[-2:]
    i = pl.program_id(0)
    k = pl.program_id(1)
    n_total = accs[0].shape[1]
    nc = _pick(n_total, (LN_COLS, LANE))
    rc = o_ref.shape[0]

    def accumulate(acc_ref):
        x = x_ref[...]
        for c in range(n_total // nc):
            cols = slice(c * nc, (c + 1) * nc)
            acc_ref[:, cols] += _dot(x, w_ref[:, cols])

    def normalise(acc_ref):
        g = g_ref[...]
        b = b_ref[...]
        base = pl.multiple_of(k * rc, rc)

        def norm_rows(lo):
            rows = pl.ds(base + lo, LN_ROWS)
            f = acc_ref[rows, :]
            acc_ref[rows, :] = jnp.zeros((LN_ROWS, n_total), F32)
            if ple:
                f = (0.5 * jnp.tanh(0.5 * f) + 0.5) * _dot(p_ref[lo:lo + LN_ROWS, :], wp_ref[...])
            y = alpha * r_ref[lo:lo + LN_ROWS, :] + f
            mu = jnp.mean(y, axis=-1, keepdims=True)
            d = y - mu
            var = jnp.mean(d * d, axis=-1, keepdims=True)
            return d * lax.rsqrt(var + LN_EPS) * g + b

        for lo0 in range(0, rc, LN_ROWS * LN_CHAINS):
            los = [lo0 + u * LN_ROWS for u in range(LN_CHAINS) if lo0 + u * LN_ROWS < rc]
            outs = [norm_rows(lo) for lo in los]
            for lo, out in zip(los, outs):
                o_ref[lo:lo + LN_ROWS, :] = out
                if emit_bf16:
                    o16_ref[lo:lo + LN_ROWS, :] = out.astype(BF16)

    @pl.when(i == 0)
    def _():
        @pl.when(k == 0)
        def _():
            for acc_ref in accs:
                acc_ref[...] = jnp.zeros_like(acc_ref)

        accumulate(accs[0])
        o_ref[...] = jnp.zeros_like(o_ref)
        if emit_bf16:
            o16_ref[...] = jnp.zeros_like(o16_ref)

    for par in range(2):
        @pl.when(jnp.logical_and(jnp.logical_and(i > 0, i < n_i), i % 2 == par))
        def _():
            accumulate(accs[par])
            normalise(accs[1 - par])

    @pl.when(i == n_i)
    def _():
        normalise(accs[(n_i - 1) % 2])


def _mm_ln(x, w, resid, g, b, alpha, *, ple=None, emit_bf16=True):
    M, K = x.shape
    N = w.shape[1]
    tm = _pick(M, (512, 256, 128))
    tk = _pick(K, (1024, 512, 256, 128))
    n_i, nk = M // tm, K // tk
    rc = tm // nk
    assert rc % LN_ROWS == 0
    kk = lambda i, k: jnp.where(i == n_i, nk - 1, k)
    chunk = lambda i, k: (jnp.maximum((i - 1) * nk + k, 0), 0)
    const = lambda i, k: (0, 0)
    in_specs = [pl.BlockSpec((tm, tk), lambda i, k: (jnp.minimum(i, n_i - 1), kk(i, k))),
                pl.BlockSpec((tk, N), lambda i, k: (kk(i, k), 0)),
                pl.BlockSpec((rc, N), chunk),
                pl.BlockSpec((1, N), const),
                pl.BlockSpec((1, N), const)]
    args = [x, w, resid, g.reshape(1, N), b.reshape(1, N)]
    if ple is not None:
        pp, wp = ple
        in_specs += [pl.BlockSpec((rc, pp.shape[1]), chunk), pl.BlockSpec(wp.shape, const)]
        args += [pp, wp]
    out_specs = [pl.BlockSpec((rc, N), chunk)]
    out_shape = [jax.ShapeDtypeStruct((M, N), F32)]
    if emit_bf16:
        out_specs.append(pl.BlockSpec((rc, N), chunk))
        out_shape.append(jax.ShapeDtypeStruct((M, N), BF16))
    outs = pl.pallas_call(
        functools.partial(_mm_ln_kernel, n_i=n_i, alpha=alpha, ple=ple is not None, emit_bf16=emit_bf16),
        grid=(n_i + 1, nk),
        in_specs=in_specs,
        out_specs=out_specs,
        out_shape=out_shape,
        scratch_shapes=[pltpu.VMEM((tm, N), F32)] * 2,
        compiler_params=_cparams(("arbitrary", "arbitrary")),
        name="mm_res_ln",
    )(*args)
    return (outs[0], outs[1]) if emit_bf16 else (outs[0], None)


def _rope64(xr, cos, s_lo, s_hi):
    return xr * cos + pltpu.roll(xr, LANE - MLA_ROPE // 2, 1) * s_lo + pltpu.roll(xr, MLA_ROPE // 2, 1) * s_hi


def _mla_up_kernel(c_ref, gq_ref, gkv_ref, wq_ref, wk_ref, wv_ref, cos_ref, slo_ref, shi_ref,
                   q_ref, k_ref, v_ref, *, q_rank, kv_rank, heads):
    cos = cos_ref[...]
    s_lo = slo_ref[...]
    s_hi = shi_ref[...]

    def rms(t, g):
        return (t * lax.rsqrt(jnp.mean(t * t, axis=-1, keepdims=True) + RMS_EPS) * g).astype(BF16)

    nq = rms(c_ref[:, :q_rank], gq_ref[...])
    qf = _dot(nq, wq_ref[...])
    for h in range(heads):
        lo = h * MLA_QK_PAD
        q_ref[:, lo:lo + MLA_NOPE] = qf[:, lo:lo + MLA_NOPE].astype(BF16)
        q_ref[:, lo + MLA_NOPE:lo + MLA_QK_PAD] = _rope64(
            qf[:, lo + MLA_NOPE:lo + MLA_QK_PAD], cos, s_lo, s_hi).astype(BF16)
    nkv = rms(c_ref[:, q_rank:q_rank + kv_rank], gkv_ref[...])
    kf = _dot(nkv, wk_ref[...])
    v_ref[...] = _dot(nkv, wv_ref[...]).astype(BF16)
    kr = _rope64(c_ref[:, q_rank + kv_rank:q_rank + kv_rank + LANE], cos, s_lo, s_hi).astype(BF16)
    for h in range(heads):
        lo = h * MLA_QK_PAD
        k_ref[:, lo:lo + MLA_NOPE] = kf[:, h * MLA_NOPE:(h + 1) * MLA_NOPE].astype(BF16)
        k_ref[:, lo + MLA_NOPE:lo + MLA_QK_PAD] = kr


def _mla_up(cbuf, c_width, gq, gkv, wq, wk, wv, tabs, seq, heads):
    M = cbuf.shape[0]
    q_rank, kv_rank = gq.shape[0], gkv.shape[0]
    tm = _pick(seq, (512, 256, 128))
    nsb = seq // tm
    const = lambda shape: pl.BlockSpec(shape, lambda i: (0, 0))
    tab = pl.BlockSpec((tm, LANE), lambda i: (i % nsb, 0))
    row = lambda n: pl.BlockSpec((tm, n), lambda i: (i, 0))
    return pl.pallas_call(
        functools.partial(_mla_up_kernel, q_rank=q_rank, kv_rank=kv_rank, heads=heads),
        grid=(M // tm,),
        in_specs=[row(c_width), const((1, q_rank)), const((1, kv_rank)),
                  const(wq.shape), const(wk.shape), const(wv.shape), tab, tab, tab],
        out_specs=[row(heads * MLA_QK_PAD), row(heads * MLA_QK_PAD), row(heads * MLA_V)],
        out_shape=[jax.ShapeDtypeStruct((M, heads * MLA_QK_PAD), BF16),
                   jax.ShapeDtypeStruct((M, heads * MLA_QK_PAD), BF16),
                   jax.ShapeDtypeStruct((M, heads * MLA_V), BF16)],
        compiler_params=_cparams(("parallel",)),
        name="mla_up",
    )(cbuf, gq.reshape(1, -1), gkv.reshape(1, -1), wq, wk, wv, *tabs)


def _attn_specs(tq, seq, dq, dk, dv, q_off, k_off, v_off):
    return [pl.BlockSpec((1, tq, dq), lambda b, h, i: (b, i, q_off + h)),
            pl.BlockSpec((1, seq, dk), lambda b, h, i: (b, 0, k_off + h)),
            pl.BlockSpec((1, seq, dv), lambda b, h, i: (b, 0, v_off + h))]


def _sb_kernel(q_ref, k_ref, v_ref, tri_ref, o_ref, *, t, scale):
    i = pl.program_id(2)
    q = q_ref[0]
    tri = tri_ref[...]
    row = lax.broadcasted_iota(jnp.int32, (t, t), 0)
    col = lax.broadcasted_iota(jnp.int32, (t, t), 1)
    past = col < row

    def block(j, c, acc, keep):
        start = pl.multiple_of(j * t, t)
        ks = k_ref[0, pl.ds(start, t), :]
        vs = v_ref[0, pl.ds(start, t), :]
        z = _dot_nt(q, ks) * scale
        log_beta = jnp.minimum(z, 0.0) - jnp.log1p(jnp.exp(-jnp.abs(z)))
        log_1m = log_beta - z
        if keep is not None:
            log_1m = jnp.where(keep, log_1m, 0.0)
        hi = log_1m.astype(BF16)
        lo = (log_1m - hi.astype(F32)).astype(BF16)
        between = _dot(hi, tri) + _dot(lo, tri) + c
        w = jnp.exp(log_beta + between)
        if keep is not None:
            w = jnp.where(keep, w, 0.0)
        acc = acc + _dot(w.astype(BF16), vs)
        c = c + jnp.sum(log_1m, axis=-1, keepdims=True)
        return c, acc

    c0 = jnp.zeros((t, 1), F32)
    acc0 = jnp.zeros((t, v_ref.shape[2]), F32)
    c, acc = block(i, c0, acc0, past)
    c, acc = block(jnp.maximum(i - 1, 0), c, acc, row < jnp.where(i > 0, t, 0))

    def cond(carry):
        jj, c_max, _, _ = carry
        return jnp.logical_and(jj < i, c_max > SB_EXP_UNDERFLOW)

    def body(carry):
        jj, _, c, acc = carry
        c, acc = block(i - 1 - jj, c, acc, None)
        return jj + 1, jnp.max(c), c, acc

    _, _, c, acc = lax.while_loop(cond, body, (jnp.int32(1), jnp.max(c), c, acc))
    o_ref[0] = acc.astype(o_ref.dtype)


def _sb_attention(qkv, batch, seq, heads, q_off, k_off, v_off):
    t = _pick(seq, (256, 128))
    arr = qkv.reshape(batch, seq, -1)
    tri = (lax.broadcasted_iota(jnp.int32, (t, t), 0) > lax.broadcasted_iota(jnp.int32, (t, t), 1)).astype(BF16)
    specs = _attn_specs(t, seq, HEAD_DIM, HEAD_DIM, HEAD_DIM, q_off, k_off, v_off)
    specs.append(pl.BlockSpec((t, t), lambda b, h, i: (0, 0)))
    return pl.pallas_call(
        functools.partial(_sb_kernel, t=t, scale=HEAD_DIM ** -0.5),
        grid=(batch, heads, seq // t),
        in_specs=specs,
        out_specs=pl.BlockSpec((1, t, HEAD_DIM), lambda b, h, i: (b, i, h)),
        out_shape=jax.ShapeDtypeStruct((batch, seq, heads * HEAD_DIM), BF16),
        compiler_params=_cparams(("parallel", "parallel", "arbitrary")),
        name="sb_attn",
    )(arr, arr, arr, tri)


def _softmax_pv(s, vs, m, l, acc, mask):
    if mask is not None:
        s = jnp.where(mask, s, -jnp.inf)
    m_new = jnp.maximum(m, jnp.max(s, axis=-1, keepdims=True))
    a = jnp.exp2(m - m_new)
    p = jnp.exp2(s - m_new)
    l = a * l + jnp.sum(p, axis=-1, keepdims=True)
    acc = a * acc + _dot(p.astype(BF16), vs)
    return m_new, l, acc


def _flash_init(t, dv):
    return jnp.full((t, 1), -jnp.inf, F32), jnp.zeros((t, 1), F32), jnp.zeros((t, dv), F32)


def _causal_sweep(i, tq, tk, step, init):
    nsub = tq // tk
    row = lax.broadcasted_iota(jnp.int32, (tq, tk), 0)
    col = lax.broadcasted_iota(jnp.int32, (tq, tk), 1)
    def body(jj, c):
        for u in range(nsub):
            c = step(jj * nsub + u, c, None)
        return c

    carry = lax.fori_loop(0, i, body, init)
    for u in range(nsub):
        carry = step(i * nsub + u, carry, col + u * tk <= row)
    return carry


def _flash_tiles(seq):
    tq = _pick(seq, (FLASH_TQ, 512, 256, 128))
    return tq, _pick(tq, (FLASH_TK, 256, 128))


def _mla_kernel(q_ref, k_ref, v_ref, o_ref, *, tq, tk, scale):
    q = q_ref[0]

    def step(j, carry, mask):
        rows = pl.ds(pl.multiple_of(j * tk, tk), tk)
        s = _dot_nt(q, k_ref[0, rows, :]) * scale
        return _softmax_pv(s, v_ref[0, rows, :], *carry, mask)

    m, l, acc = _causal_sweep(pl.program_id(2), tq, tk, step, _flash_init(tq, v_ref.shape[2]))
    o_ref[0] = (acc / l).astype(o_ref.dtype)


def _mla_attention(q, k, v, batch, seq, heads):
    t, tk = _flash_tiles(seq)
    return pl.pallas_call(
        functools.partial(_mla_kernel, tq=t, tk=tk, scale=(MLA_NOPE + MLA_ROPE) ** -0.5 * LOG2E),
        grid=(batch, heads, seq // t),
        in_specs=_attn_specs(t, seq, MLA_QK_PAD, MLA_QK_PAD, MLA_V, 0, 0, 0),
        out_specs=pl.BlockSpec((1, t, MLA_V), lambda b, h, i: (b, i, h)),
        out_shape=jax.ShapeDtypeStruct((batch, seq, heads * MLA_V), BF16),
        compiler_params=_cparams(("parallel", "parallel", "arbitrary")),
        name="mla_attn",
    )(q.reshape(batch, seq, -1), k.reshape(batch, seq, -1), v.reshape(batch, seq, -1))


def _diff_kernel(q_ref, k_ref, v_ref, lam_ref, g_ref, o_ref, *, tq, tk, scale, lam_init):
    d = HEAD_DIM
    q1 = q_ref[0, :, :d]
    q2 = q_ref[0, :, d:]

    def step(j, carry, mask):
        rows = pl.ds(pl.multiple_of(j * tk, tk), tk)
        vs = v_ref[0, rows, :]
        s1 = _dot_nt(q1, k_ref[0, rows, :d]) * scale
        s2 = _dot_nt(q2, k_ref[0, rows, d:]) * scale
        return _softmax_pv(s1, vs, *carry[:3], mask) + _softmax_pv(s2, vs, *carry[3:], mask)

    one = _flash_init(tq, v_ref.shape[2])
    m1, l1, a1, m2, l2, a2 = _causal_sweep(pl.program_id(2), tq, tk, step, one + one)
    lp = lam_ref[...]
    lam = (jnp.exp(jnp.sum(lp[0:1] * lp[1:2], axis=-1, keepdims=True))
           - jnp.exp(jnp.sum(lp[2:3] * lp[3:4], axis=-1, keepdims=True)) + lam_init)
    o = a1 / l1 - lam * (a2 / l2)
    o = o * lax.rsqrt(jnp.mean(o * o, axis=-1, keepdims=True) + RMS_EPS) * g_ref[...]
    o_ref[0] = (o * (1.0 - lam_init)).astype(o_ref.dtype)


def _diff_attention(qk, q_off, k_off, vbuf, v_off, lam_params, subln_g, lam_init, batch, seq, heads):
    t, tk = _flash_tiles(seq)
    dd = 2 * HEAD_DIM
    specs = [pl.BlockSpec((1, t, dd), lambda b, h, i: (b, i, q_off + h)),
             pl.BlockSpec((1, seq, dd), lambda b, h, i: (b, 0, k_off + h)),
             pl.BlockSpec((1, seq, dd), lambda b, h, i: (b, 0, v_off + h)),
             pl.BlockSpec(lam_params.shape, lambda b, h, i: (0, 0)),
             pl.BlockSpec((1, dd), lambda b, h, i: (0, 0))]
    return pl.pallas_call(
        functools.partial(_diff_kernel, tq=t, tk=tk, scale=HEAD_DIM ** -0.5 * LOG2E, lam_init=lam_init),
        grid=(batch, heads, seq // t),
        in_specs=specs,
        out_specs=pl.BlockSpec((1, t, dd), lambda b, h, i: (b, i, h)),
        out_shape=jax.ShapeDtypeStruct((batch, seq, heads * dd), BF16),
        compiler_params=_cparams(("parallel", "parallel", "arbitrary")),
        name="diff_attn",
    )(qk.reshape(batch, seq, -1), qk.reshape(batch, seq, -1), vbuf.reshape(batch, seq, -1),
      lam_params, subln_g.reshape(1, dd))


def _dil_kernel(q_ref, kp_ref, kc_ref, vp_ref, vc_ref, o_ref, acc_scr, m_scr, l_scr, *, scale):
    has_prev = pl.program_id(2) > 0
    ri = lax.broadcasted_iota(jnp.int32, (QBLOCK, QBLOCK), 0)
    ci = lax.broadcasted_iota(jnp.int32, (QBLOCK, QBLOCK), 1)
    ri_first = ri + jnp.where(has_prev, 0, QBLOCK)

    def rows(n, r, dil):
        start = n * QBLOCK * dil + r
        return pl.ds(start, QBLOCK) if dil == 1 else pl.ds(start, QBLOCK, stride=dil)

    def bqk(a, b):
        return lax.dot_general(a, b, (((2,), (2,)), ((0,), (0,))), preferred_element_type=F32)

    def bpv(a, b):
        return lax.dot_general(a, b, (((2,), (1,)), ((0,), (0,))), preferred_element_type=F32)

    for bi, (window, dil) in enumerate(DIL_BRANCHES):
        assert window // dil == QBLOCK
        nblk = DIL_CHUNK // (QBLOCK * dil)
        units = [(n, r) for r in range(dil) for n in range(nblk)]
        loaded = {}
        for g in range(0, len(units), DIL_UNITS):
            group = units[g:g + DIL_UNITS]
            qs, kcs, vcs, kps, vps, thr = [], [], [], [], [], []
            for n, r in group:
                cur = rows(n, r, dil)
                qs.append(q_ref[0, cur, :].astype(BF16))
                loaded[n, r] = (kc_ref[0, cur, :].astype(BF16), vc_ref[0, cur, :].astype(BF16))
                kcs.append(loaded[n, r][0])
                vcs.append(loaded[n, r][1])
                if n > 0:
                    kp, vp = loaded.pop((n - 1, r))
                    thr.append(ri)
                else:
                    prv = rows(nblk - 1, r, dil)
                    kp, vp = kp_ref[0, prv, :].astype(BF16), vp_ref[0, prv, :].astype(BF16)
                    thr.append(ri_first)
                kps.append(kp)
                vps.append(vp)
            q = jnp.stack(qs)
            ci3 = jnp.broadcast_to(ci, (len(group), QBLOCK, QBLOCK))
            s_p = jnp.where(ci3 >= jnp.stack(thr), bqk(q, jnp.stack(kps)) * scale, -jnp.inf)
            s_c = jnp.where(ci3 <= jnp.broadcast_to(ri, ci3.shape), bqk(q, jnp.stack(kcs)) * scale, -jnp.inf)
            mx = jnp.max(jnp.maximum(s_p, s_c), axis=-1, keepdims=True)
            mx = jnp.broadcast_to(mx, ci3.shape)
            e_p = jnp.exp2(s_p - mx)
            e_c = jnp.exp2(s_c - mx)
            lsum = jnp.broadcast_to(jnp.sum(e_p + e_c, axis=-1, keepdims=True), ci3.shape)
            pv = bpv(e_p.astype(BF16), jnp.stack(vps)) + bpv(e_c.astype(BF16), jnp.stack(vcs))
            for u, (n, r) in enumerate(group):
                cur = rows(n, r, dil)
                acc_scr[bi, cur, :] = pv[u]
                m_scr[bi, cur, :] = mx[u]
                l_scr[bi, cur, :] = lsum[u]

    nb = len(DIL_BRANCHES)
    for c in range(DIL_CHUNK // DIL_MERGE_ROWS):
        rs = pl.ds(c * DIL_MERGE_ROWS, DIL_MERGE_ROWS)
        ms = [m_scr[b, rs, :] for b in range(nb)]
        top = functools.reduce(jnp.maximum, ms)
        ws = [jnp.exp2(m - top) for m in ms]
        num = sum(w * acc_scr[b, rs, :] for b, w in enumerate(ws))
        den = sum(w * l_scr[b, rs, :] for b, w in enumerate(ws))
        o_ref[0, rs, :] = (num / den).astype(o_ref.dtype)


def _dil_attention(qk, q_off, k_off, vbuf, v_off, batch, seq, heads):
    assert seq % DIL_CHUNK == 0
    qk3 = qk.reshape(batch, seq, -1)
    v3 = vbuf.reshape(batch, seq, -1)
    blk = (1, DIL_CHUNK, HEAD_DIM)
    cur = lambda off: pl.BlockSpec(blk, lambda b, h, c: (b, c, off + h))
    prev = lambda off: pl.BlockSpec(blk, lambda b, h, c: (b, jnp.maximum(c - 1, 0), off + h))
    return pl.pallas_call(
        functools.partial(_dil_kernel, scale=HEAD_DIM ** -0.5 * LOG2E),
        grid=(batch, heads, seq // DIL_CHUNK),
        in_specs=[cur(q_off), prev(k_off), cur(k_off), prev(v_off), cur(v_off)],
        out_specs=pl.BlockSpec(blk, lambda b, h, c: (b, c, h)),
        out_shape=jax.ShapeDtypeStruct((batch, seq, heads * HEAD_DIM), BF16),
        scratch_shapes=[pltpu.VMEM((len(DIL_BRANCHES), DIL_CHUNK, LANE), F32)] * 3,
        compiler_params=_cparams(("parallel", "parallel", "arbitrary")),
        name="dil_attn",
    )(qk3, qk3, qk3, v3, v3)


def _rope_tables(seq):
    pos = jnp.arange(seq, dtype=jnp.int32).astype(F32)

    def cs(width):
        half = width // 2
        inv = ROPE_THETA ** (-jnp.arange(half, dtype=F32) / half)
        ang = pos[:, None] * inv[None, :]
        return jnp.cos(ang), jnp.sin(ang)

    cos, sin = cs(HEAD_DIM)
    full = (jnp.concatenate([cos, cos], axis=-1), jnp.concatenate([-sin, sin], axis=-1))
    cos, sin = cs(MLA_ROPE)
    z = jnp.zeros_like(cos)
    pad = jnp.zeros((seq, LANE - MLA_ROPE), F32)
    mla = (jnp.concatenate([cos, cos, pad], axis=-1),
           jnp.concatenate([-sin, z, pad], axis=-1),
           jnp.concatenate([z, sin, pad], axis=-1))
    return full, mla


def _layer(xf, xb, p_i, lam_init, alpha, tabs, dims, w_in, w_o, mla_q_norm, mla_w_uq, mla_kv_norm, mla_w_ukv,
           diff_lambda, diff_subln, ln_attn_g, ln_attn_b, w_ff1, w_ff2, ln_ff_g, ln_ff_b,
           w_ple_gate, w_ple_proj, ln_ple_g, ln_ple_b, last):
    batch, seq, d_model = dims
    rope_full, rope_mla = tabs
    mix_heads = d_model // HEAD_DIM
    sb_heads = dil_heads = mla_heads = mix_heads // 4
    diff_heads = mix_heads // 8
    q_rank, kv_rank = mla_q_norm.shape[0], mla_kv_norm.shape[0]
    hw = sb_heads * HEAD_DIM
    widths = (hw,) * 3 + (hw,) * 3 + (q_rank, kv_rank, MLA_ROPE) + (diff_heads * 2 * HEAD_DIM,) * 3
    offs = [0]
    for wd in widths:
        offs.append(offs[-1] + wd)
    col = lambda n: w_in[:, offs[n]:offs[n + 1]]
    a_q, a_k, a_v, b_q, b_k, b_v, c_q, c_kv, c_kr, d_q, d_k, d_v = (col(n) for n in range(12))

    w_plain = jnp.concatenate([a_q, a_k, a_v, d_v], axis=1).astype(BF16)
    w_rope_f = jnp.concatenate([b_q, b_k], axis=1).astype(BF16)
    w_rope_h = jnp.concatenate([d_q, d_k], axis=1).astype(BF16)
    c_used = q_rank + kv_rank + LANE
    c_width = -(-c_used // 1024) * 1024
    w_c = jnp.concatenate([c_q, c_kv, c_kr, jnp.zeros((d_model, c_width - c_used + LANE - MLA_ROPE), w_in.dtype),
                           b_v], axis=1).astype(BF16)

    h_plain = _mm(xb, w_plain, BF16)
    h_rope_f = _mm(xb, w_rope_f, F32, rope=rope_full, seq=seq)
    h_rope_h = _mm(xb, w_rope_h, BF16, rope=rope_full, seq=seq)
    h_c = _mm(xb, w_c, F32)

    y_a = _sb_attention(h_plain, batch, seq, sb_heads, 0, sb_heads, 2 * sb_heads)
    y_b = _dil_attention(h_rope_f, 0, dil_heads, h_c, c_width // HEAD_DIM, batch, seq, dil_heads)

    wq = mla_w_uq.reshape(q_rank, mla_heads, MLA_NOPE + MLA_ROPE)
    wq = jnp.pad(wq, ((0, 0), (0, 0), (0, MLA_QK_PAD - MLA_NOPE - MLA_ROPE))).reshape(q_rank, -1).astype(BF16)
    wkv = mla_w_ukv.reshape(kv_rank, mla_heads, MLA_NOPE + MLA_V)
    wk = wkv[:, :, :MLA_NOPE].reshape(kv_rank, -1).astype(BF16)
    wv = wkv[:, :, MLA_NOPE:].reshape(kv_rank, -1).astype(BF16)
    mq, mk, mv = _mla_up(h_c, c_width, mla_q_norm, mla_kv_norm, wq, wk, wv, rope_mla, seq, mla_heads)
    y_c = _mla_attention(mq, mk, mv, batch, seq, mla_heads)

    y_d = _diff_attention(h_rope_h, 0, diff_heads, h_plain, 3 * sb_heads // 2, diff_lambda, diff_subln,
                          lam_init, batch, seq, diff_heads)

    mix = jnp.concatenate([y_a, y_b, y_c, y_d], axis=-1).reshape(batch * seq, -1)
    xf, xb = _mm_ln(mix, w_o.astype(BF16), xf, ln_attn_g, ln_attn_b, alpha)

    u = _mm_ws_relu2(xb, *w_ff1)
    xf, xb = _mm_ln(u, w_ff2.astype(BF16), xf, ln_ff_g, ln_ff_b, alpha)

    xf, xb = _mm_ln(xb, w_ple_gate.astype(BF16), xf, ln_ple_g, ln_ple_b, alpha,
                    ple=(p_i.astype(BF16), w_ple_proj.astype(BF16)), emit_bf16=not last)
    return xf, xb


def kernel(x, p, w_in, w_o, mla_q_norm, mla_w_uq, mla_kv_norm, mla_w_ukv, diff_lambda, diff_subln,
           ln_attn_g, ln_attn_b, w_ff1, w_ff2, ln_ff_g, ln_ff_b, w_ple_gate, w_ple_proj, ln_ple_g, ln_ple_b):
    batch, seq, d_model = x.shape
    depth = w_in.shape[0]
    alpha = (2 * depth) ** 0.25
    tabs = _rope_tables(seq)
    xf = x.reshape(batch * seq, d_model)
    xb = xf.astype(BF16)
    for i in range(depth):
        lam_init = 0.8 - 0.6 * math.exp(-0.3 * i)
        xf, xb = _layer(xf, xb, p[i].reshape(batch * seq, -1), lam_init, alpha, tabs, (batch, seq, d_model),
                        w_in[i], w_o[i], mla_q_norm[i], mla_w_uq[i], mla_kv_norm[i], mla_w_ukv[i],
                        diff_lambda[i], diff_subln[i], ln_attn_g[i], ln_attn_b[i], (w_ff1, i), w_ff2[i],
                        ln_ff_g[i], ln_ff_b[i], w_ple_gate[i], w_ple_proj[i], ln_ple_g[i], ln_ple_b[i],
                        last=i == depth - 1)
    return xf.reshape(batch, seq, d_model)
```

```python
import functools
import math

import jax
import jax.numpy as jnp
from jax import lax
from jax.experimental import pallas as pl
from jax.experimental.pallas import tpu as pltpu

F32 = jnp.float32
BF16 = jnp.bfloat16

LANE = 128
HEAD_DIM = 128
QBLOCK = 128
ROPE_THETA = 10000.0
LN_EPS = 1e-5
RMS_EPS = 1e-6
MLA_NOPE = 128
MLA_ROPE = 64
MLA_V = 128
MLA_QK_PAD = 256
DIL_BRANCHES = ((128, 1), (512, 4), (2048, 16))
DIL_CHUNK = QBLOCK * 16
VMEM_LIMIT_BYTES = 56 * 1024 * 1024
SB_EXP_UNDERFLOW = -104.0
LOG2E = 1.4426950408889634
FLASH_TQ = 1024
FLASH_TK = 512
DIL_MERGE_ROWS = 256
DIL_UNITS = 16


def _pick(n, cands):
    for c in cands:
        if n % c == 0:
            return c
    return n


def _cparams(sem):
    return pltpu.CompilerParams(dimension_semantics=sem, vmem_limit_bytes=VMEM_LIMIT_BYTES)


def _dot(a, b):
    return jnp.dot(a, b, preferred_element_type=F32)


def _dot_nt(a, b):
    return lax.dot_general(a, b, (((1,), (1,)), ((), ())), preferred_element_type=F32)


def _mm_kernel(*refs, rope):
    if rope:
        x_ref, w_ref, cos_ref, sin_ref, o_ref = refs
    else:
        x_ref, w_ref, o_ref = refs
    acc = _dot(x_ref[...], w_ref[...])
    if rope:
        cos = cos_ref[...]
        sin = sin_ref[...]
        for c in range(acc.shape[1] // LANE):
            blk = acc[:, c * LANE:(c + 1) * LANE]
            rot = pltpu.roll(blk, LANE // 2, 1)
            o_ref[:, c * LANE:(c + 1) * LANE] = (blk * cos + rot * sin).astype(o_ref.dtype)
    else:
        o_ref[...] = acc.astype(o_ref.dtype)


def _mm(x, w, out_dtype, *, rope=None, seq=None):
    M, K = x.shape
    N = w.shape[1]
    tm = _pick(M, (1024, 512, 256, 128))
    tn = _pick(N, (1024, 512, 256, 128))
    in_specs = [pl.BlockSpec((tm, K), lambda i, j: (i, 0)),
                pl.BlockSpec((K, tn), lambda i, j: (0, j))]
    args = [x, w]
    if rope is not None:
        tm = _pick(seq, (tm, 512, 256, 128))
        in_specs[0] = pl.BlockSpec((tm, K), lambda i, j: (i, 0))
        nsb = seq // tm
        tab = pl.BlockSpec((tm, LANE), lambda i, j: (i % nsb, 0))
        in_specs += [tab, tab]
        args += list(rope)
    return pl.pallas_call(
        functools.partial(_mm_kernel, rope=rope is not None),
        grid=(M // tm, N // tn),
        in_specs=in_specs,
        out_specs=pl.BlockSpec((tm, tn), lambda i, j: (i, j)),
        out_shape=jax.ShapeDtypeStruct((M, N), out_dtype),
        compiler_params=_cparams(("parallel", "parallel")),
        name="proj_mm",
    )(*args)


WS_CAST_ROWS = 512


def _mm_ws_kernel(x_ref, w_ref, o_ref, wb_ref):
    @pl.when(pl.program_id(1) == 0)
    def _():
        for r in range(0, w_ref.shape[0], WS_CAST_ROWS):
            wb_ref[r:r + WS_CAST_ROWS, :] = w_ref[r:r + WS_CAST_ROWS, :].astype(BF16)

    acc = _dot(x_ref[...], wb_ref[...])
    o_ref[...] = jnp.square(jnp.maximum(acc, 0.0)).astype(o_ref.dtype)


def _mm_ws_relu2(x, w_stack, layer):
    M, K = x.shape
    N = w_stack.shape[2]
    tm = _pick(M, (512, 256, 128))
    tn = _pick(N, (1024, 512, 256, 128))
    assert K % WS_CAST_ROWS == 0
    return pl.pallas_call(
        _mm_ws_kernel,
        grid=(N // tn, M // tm),
        in_specs=[pl.BlockSpec((tm, K), lambda j, i: (i, 0)),
                  pl.BlockSpec((None, K, tn), lambda j, i: (layer, 0, j))],
        out_specs=pl.BlockSpec((tm, tn), lambda j, i: (i, j)),
        out_shape=jax.ShapeDtypeStruct((M, N), BF16),
        scratch_shapes=[pltpu.VMEM((K, tn), BF16)],
        compiler_params=_cparams(("arbitrary", "arbitrary")),
        name="ff1_mm",
    )(x, w_stack)


LN_ROWS = 32
LN_CHAINS = 2
LN_COLS = 1024


def _mm_ln_kernel(*refs, n_i, alpha, ple, emit_bf16):
    refs = list(refs)
    x_ref, w_ref, r_ref, g_ref, b_ref = refs[:5]
    rest = refs[5:]
    if ple:
        p_ref, wp_ref = rest[:2]
        rest = rest[2:]
    o_ref = rest[0]
    o16_ref = rest[1] if emit_bf16 else None
    accs = rest[-2:]
    i = pl.program_id(0)
    k = pl.program_id(1)
    n_total = accs[0].shape[1]
    nc = _pick(n_total, (LN_COLS, LANE))
    rc = o_ref.shape[0]

    def accumulate(acc_ref):
        x = x_ref[...]
        for c in range(n_total // nc):
            cols = slice(c * nc, (c + 1) * nc)
            acc_ref[:, cols] += _dot(x, w_ref[:, cols])

    def normalise(acc_ref):
        g = g_ref[...]
        b = b_ref[...]
        base = pl.multiple_of(k * rc, rc)

        def norm_rows(lo):
            rows = pl.ds(base + lo, LN_ROWS)
            f = acc_ref[rows, :]
            acc_ref[rows, :] = jnp.zeros((LN_ROWS, n_total), F32)
            if ple:
                f = (0.5 * jnp.tanh(0.5 * f) + 0.5) * _dot(p_ref[lo:lo + LN_ROWS, :], wp_ref[...])
            y = alpha * r_ref[lo:lo + LN_ROWS, :] + f
            mu = jnp.mean(y, axis=-1, keepdims=True)
            d = y - mu
            var = jnp.mean(d * d, axis=-1, keepdims=True)
            return d * lax.rsqrt(var + LN_EPS) * g + b

        for lo0 in range(0, rc, LN_ROWS * LN_CHAINS):
            los = [lo0 + u * LN_ROWS for u in range(LN_CHAINS) if lo0 + u * LN_ROWS < rc]
            outs = [norm_rows(lo) for lo in los]
            for lo, out in zip(los, outs):
                o_ref[lo:lo + LN_ROWS, :] = out
                if emit_bf16:
                    o16_ref[lo:lo + LN_ROWS, :] = out.astype(BF16)

    @pl.when(i == 0)
    def _():
        @pl.when(k == 0)
        def _():
            for acc_ref in accs:
                acc_ref[...] = jnp.zeros_like(acc_ref)

        accumulate(accs[0])
        o_ref[...] = jnp.zeros_like(o_ref)
        if emit_bf16:
            o16_ref[...] = jnp.zeros_like(o16_ref)

    for par in range(2):
        @pl.when(jnp.logical_and(jnp.logical_and(i > 0, i < n_i), i % 2 == par))
        def _():
            accumulate(accs[par])
            normalise(accs[1 - par])

    @pl.when(i == n_i)
    def _():
        normalise(accs[(n_i - 1) % 2])


def _mm_ln(x, w, resid, g, b, alpha, *, ple=None, emit_bf16=True):
    M, K = x.shape
    N = w.shape[1]
    tm = _pick(M, (512, 256, 128))
    tk = _pick(K, (1024, 512, 256, 128))
    n_i, nk = M // tm, K // tk
    rc = tm // nk
    assert rc % LN_ROWS == 0
    kk = lambda i, k: jnp.where(i == n_i, nk - 1, k)
    chunk = lambda i, k: (jnp.maximum((i - 1) * nk + k, 0), 0)
    const = lambda i, k: (0, 0)
    in_specs = [pl.BlockSpec((tm, tk), lambda i, k: (jnp.minimum(i, n_i - 1), kk(i, k))),
                pl.BlockSpec((tk, N), lambda i, k: (kk(i, k), 0)),
                pl.BlockSpec((rc, N), chunk),
                pl.BlockSpec((1, N), const),
                pl.BlockSpec((1, N), const)]
    args = [x, w, resid, g.reshape(1, N), b.reshape(1, N)]
    if ple is not None:
        pp, wp = ple
        in_specs += [pl.BlockSpec((rc, pp.shape[1]), chunk), pl.BlockSpec(wp.shape, const)]
        args += [pp, wp]
    out_specs = [pl.BlockSpec((rc, N), chunk)]
    out_shape = [jax.ShapeDtypeStruct((M, N), F32)]
    if emit_bf16:
        out_specs.append(pl.BlockSpec((rc, N), chunk))
        out_shape.append(jax.ShapeDtypeStruct((M, N), BF16))
    outs = pl.pallas_call(
        functools.partial(_mm_ln_kernel, n_i=n_i, alpha=alpha, ple=ple is not None, emit_bf16=emit_bf16),
        grid=(n_i + 1, nk),
        in_specs=in_specs,
        out_specs=out_specs,
        out_shape=out_shape,
        scratch_shapes=[pltpu.VMEM((tm, N), F32)] * 2,
        compiler_params=_cparams(("arbitrary", "arbitrary")),
        name="mm_res_ln",
    )(*args)
    return (outs[0], outs[1]) if emit_bf16 else (outs[0], None)


def _rope64(xr, cos, s_lo, s_hi):
    return xr * cos + pltpu.roll(xr, LANE - MLA_ROPE // 2, 1) * s_lo + pltpu.roll(xr, MLA_ROPE // 2, 1) * s_hi


def _mla_up_kernel(c_ref, gq_ref, gkv_ref, wq_ref, wk_ref, wv_ref, cos_ref, slo_ref, shi_ref,
                   q_ref, k_ref, v_ref, *, q_rank, kv_rank, heads):
    cos = cos_ref[...]
    s_lo = slo_ref[...]
    s_hi = shi_ref[...]

    def rms(t, g):
        return (t * lax.rsqrt(jnp.mean(t * t, axis=-1, keepdims=True) + RMS_EPS) * g).astype(BF16)

    nq = rms(c_ref[:, :q_rank], gq_ref[...])
    qf = _dot(nq, wq_ref[...])
    for h in range(heads):
        lo = h * MLA_QK_PAD
        q_ref[:, lo:lo + MLA_NOPE] = qf[:, lo:lo + MLA_NOPE].astype(BF16)
        q_ref[:, lo + MLA_NOPE:lo + MLA_QK_PAD] = _rope64(
            qf[:, lo + MLA_NOPE:lo + MLA_QK_PAD], cos, s_lo, s_hi).astype(BF16)
    nkv = rms(c_ref[:, q_rank:q_rank + kv_rank], gkv_ref[...])
    kf = _dot(nkv, wk_ref[...])
    v_ref[...] = _dot(nkv, wv_ref[...]).astype(BF16)
    kr = _rope64(c_ref[:, q_rank + kv_rank:q_rank + kv_rank + LANE], cos, s_lo, s_hi).astype(BF16)
    for h in range(heads):
        lo = h * MLA_QK_PAD
        k_ref[:, lo:lo + MLA_NOPE] = kf[:, h * MLA_NOPE:(h + 1) * MLA_NOPE].astype(BF16)
        k_ref[:, lo + MLA_NOPE:lo + MLA_QK_PAD] = kr


def _mla_up(cbuf, c_width, gq, gkv, wq, wk, wv, tabs, seq, heads):
    M = cbuf.shape[0]
    q_rank, kv_rank = gq.shape[0], gkv.shape[0]
    tm = _pick(seq, (512, 256, 128))
    nsb = seq // tm
    const = lambda shape: pl.BlockSpec(shape, lambda i: (0, 0))
    tab = pl.BlockSpec((tm, LANE), lambda i: (i % nsb, 0))
    row = lambda n: pl.BlockSpec((tm, n), lambda i: (i, 0))
    return pl.pallas_call(
        functools.partial(_mla_up_kernel, q_rank=q_rank, kv_rank=kv_rank, heads=heads),
        grid=(M // tm,),
        in_specs=[row(c_width), const((1, q_rank)), const((1, kv_rank)),
                  const(wq.shape), const(wk.shape), const(wv.shape), tab, tab, tab],
        out_specs=[row(heads * MLA_QK_PAD), row(heads * MLA_QK_PAD), row(heads * MLA_V)],
        out_shape=[jax.ShapeDtypeStruct((M, heads * MLA_QK_PAD), BF16),
                   jax.ShapeDtypeStruct((M, heads * MLA_QK_PAD), BF16),
                   jax.ShapeDtypeStruct((M, heads * MLA_V), BF16)],
        compiler_params=_cparams(("parallel",)),
        name="mla_up",
    )(cbuf, gq.reshape(1, -1), gkv.reshape(1, -1), wq, wk, wv, *tabs)


def _attn_specs(tq, seq, dq, dk, dv, q_off, k_off, v_off):
    return [pl.BlockSpec((1, tq, dq), lambda b, h, i: (b, i, q_off + h)),
            pl.BlockSpec((1, seq, dk), lambda b, h, i: (b, 0, k_off + h)),
            pl.BlockSpec((1, seq, dv), lambda b, h, i: (b, 0, v_off + h))]


def _sb_kernel(q_ref, k_ref, v_ref, tri_ref, o_ref, *, t, scale):
    i = pl.program_id(2)
    q = q_ref[0]
    tri = tri_ref[...]
    row = lax.broadcasted_iota(jnp.int32, (t, t), 0)
    col = lax.broadcasted_iota(jnp.int32, (t, t), 1)
    past = col < row

    def block(j, c, acc, keep):
        start = pl.multiple_of(j * t, t)
        ks = k_ref[0, pl.ds(start, t), :]
        vs = v_ref[0, pl.ds(start, t), :]
        z = _dot_nt(q, ks) * scale
        log_beta = jnp.minimum(z, 0.0) - jnp.log1p(jnp.exp(-jnp.abs(z)))
        log_1m = log_beta - z
        if keep is not None:
            log_1m = jnp.where(keep, log_1m, 0.0)
        hi = log_1m.astype(BF16)
        lo = (log_1m - hi.astype(F32)).astype(BF16)
        between = _dot(hi, tri) + _dot(lo, tri) + c
        w = jnp.exp(log_beta + between)
        if keep is not None:
            w = jnp.where(keep, w, 0.0)
        acc = acc + _dot(w.astype(BF16), vs)
        c = c + jnp.sum(log_1m, axis=-1, keepdims=True)
        return c, acc

    c0 = jnp.zeros((t, 1), F32)
    acc0 = jnp.zeros((t, v_ref.shape[2]), F32)
    c, acc = block(i, c0, acc0, past)
    c, acc = block(jnp.maximum(i - 1, 0), c, acc, row < jnp.where(i > 0, t, 0))

    def cond(carry):
        jj, c_max, _, _ = carry
        return jnp.logical_and(jj < i, c_max > SB_EXP_UNDERFLOW)

    def body(carry):
        jj, _, c, acc = carry
        c, acc = block(i - 1 - jj, c, acc, None)
        return jj + 1, jnp.max(c), c, acc

    _, _, c, acc = lax.while_loop(cond, body, (jnp.int32(1), jnp.max(c), c, acc))
    o_ref[0] = acc.astype(o_ref.dtype)


def _sb_attention(qkv, batch, seq, heads, q_off, k_off, v_off):
    t = _pick(seq, (256, 128))
    arr = qkv.reshape(batch, seq, -1)
    tri = (lax.broadcasted_iota(jnp.int32, (t, t), 0) > lax.broadcasted_iota(jnp.int32, (t, t), 1)).astype(BF16)
    specs = _attn_specs(t, seq, HEAD_DIM, HEAD_DIM, HEAD_DIM, q_off, k_off, v_off)
    specs.append(pl.BlockSpec((t, t), lambda b, h, i: (0, 0)))
    return pl.pallas_call(
        functools.partial(_sb_kernel, t=t, scale=HEAD_DIM ** -0.5),
        grid=(batch, heads, seq // t),
        in_specs=specs,
        out_specs=pl.BlockSpec((1, t, HEAD_DIM), lambda b, h, i: (b, i, h)),
        out_shape=jax.ShapeDtypeStruct((batch, seq, heads * HEAD_DIM), BF16),
        compiler_params=_cparams(("parallel", "parallel", "arbitrary")),
        name="sb_attn",
    )(arr, arr, arr, tri)


def _softmax_pv(s, vs, m, l, acc, mask):
    if mask is not None:
        s = jnp.where(mask, s, -jnp.inf)
    m_new = jnp.maximum(m, jnp.max(s, axis=-1, keepdims=True))
    a = jnp.exp2(m - m_new)
    p = jnp.exp2(s - m_new)
    l = a * l + jnp.sum(p, axis=-1, keepdims=True)
    acc = a * acc + _dot(p.astype(BF16), vs)
    return m_new, l, acc


def _flash_init(t, dv):
    return jnp.full((t, 1), -jnp.inf, F32), jnp.zeros((t, 1), F32), jnp.zeros((t, dv), F32)


def _causal_sweep(i, tq, tk, step, init):
    nsub = tq // tk
    row = lax.broadcasted_iota(jnp.int32, (tq, tk), 0)
    col = lax.broadcasted_iota(jnp.int32, (tq, tk), 1)
    def body(jj, c):
        for u in range(nsub):
            c = step(jj * nsub + u, c, None)
        return c

    carry = lax.fori_loop(0, i, body, init)
    for u in range(nsub):
        carry = step(i * nsub + u, carry, col + u * tk <= row)
    return carry


def _flash_tiles(seq):
    tq = _pick(seq, (FLASH_TQ, 512, 256, 128))
    return tq, _pick(tq, (FLASH_TK, 256, 128))


def _mla_kernel(q_ref, k_ref, v_ref, o_ref, *, tq, tk, scale):
    q = q_ref[0]

    def step(j, carry, mask):
        rows = pl.ds(pl.multiple_of(j * tk, tk), tk)
        s = _dot_nt(q, k_ref[0, rows, :]) * scale
        return _softmax_pv(s, v_ref[0, rows, :], *carry, mask)

    m, l, acc = _causal_sweep(pl.program_id(2), tq, tk, step, _flash_init(tq, v_ref.shape[2]))
    o_ref[0] = (acc / l).astype(o_ref.dtype)


def _mla_attention(q, k, v, batch, seq, heads):
    t, tk = _flash_tiles(seq)
    return pl.pallas_call(
        functools.partial(_mla_kernel, tq=t, tk=tk, scale=(MLA_NOPE + MLA_ROPE) ** -0.5 * LOG2E),
        grid=(batch, heads, seq // t),
        in_specs=_attn_specs(t, seq, MLA_QK_PAD, MLA_QK_PAD, MLA_V, 0, 0, 0),
        out_specs=pl.BlockSpec((1, t, MLA_V), lambda b, h, i: (b, i, h)),
        out_shape=jax.ShapeDtypeStruct((batch, seq, heads * MLA_V), BF16),
        compiler_params=_cparams(("parallel", "parallel", "arbitrary")),
        name="mla_attn",
    )(q.reshape(batch, seq, -1), k.reshape(batch, seq, -1), v.reshape(batch, seq, -1))


def _diff_kernel(q_ref, k_ref, v_ref, lam_ref, g_ref, o_ref, *, tq, tk, scale, lam_init):
    d = HEAD_DIM
    q1 = q_ref[0, :, :d]
    q2 = q_ref[0, :, d:]

    def step(j, carry, mask):
        rows = pl.ds(pl.multiple_of(j * tk, tk), tk)
        vs = v_ref[0, rows, :]
        s1 = _dot_nt(q1, k_ref[0, rows, :d]) * scale
        s2 = _dot_nt(q2, k_ref[0, rows, d:]) * scale
        return _softmax_pv(s1, vs, *carry[:3], mask) + _softmax_pv(s2, vs, *carry[3:], mask)

    one = _flash_init(tq, v_ref.shape[2])
    m1, l1, a1, m2, l2, a2 = _causal_sweep(pl.program_id(2), tq, tk, step, one + one)
    lp = lam_ref[...]
    lam = (jnp.exp(jnp.sum(lp[0:1] * lp[1:2], axis=-1, keepdims=True))
           - jnp.exp(jnp.sum(lp[2:3] * lp[3:4], axis=-1, keepdims=True)) + lam_init)
    o = a1 / l1 - lam * (a2 / l2)
    o = o * lax.rsqrt(jnp.mean(o * o, axis=-1, keepdims=True) + RMS_EPS) * g_ref[...]
    o_ref[0] = (o * (1.0 - lam_init)).astype(o_ref.dtype)


def _diff_attention(qk, q_off, k_off, vbuf, v_off, lam_params, subln_g, lam_init, batch, seq, heads):
    t, tk = _flash_tiles(seq)
    dd = 2 * HEAD_DIM
    specs = [pl.BlockSpec((1, t, dd), lambda b, h, i: (b, i, q_off + h)),
             pl.BlockSpec((1, seq, dd), lambda b, h, i: (b, 0, k_off + h)),
             pl.BlockSpec((1, seq, dd), lambda b, h, i: (b, 0, v_off + h)),
             pl.BlockSpec(lam_params.shape, lambda b, h, i: (0, 0)),
             pl.BlockSpec((1, dd), lambda b, h, i: (0, 0))]
    return pl.pallas_call(
        functools.partial(_diff_kernel, tq=t, tk=tk, scale=HEAD_DIM ** -0.5 * LOG2E, lam_init=lam_init),
        grid=(batch, heads, seq // t),
        in_specs=specs,
        out_specs=pl.BlockSpec((1, t, dd), lambda b, h, i: (b, i, h)),
        out_shape=jax.ShapeDtypeStruct((batch, seq, heads * dd), BF16),
        compiler_params=_cparams(("parallel", "parallel", "arbitrary")),
        name="diff_attn",
    )(qk.reshape(batch, seq, -1), qk.reshape(batch, seq, -1), vbuf.reshape(batch, seq, -1),
      lam_params, subln_g.reshape(1, dd))


def _dil_kernel(q_ref, kp_ref, kc_ref, vp_ref, vc_ref, o_ref, acc_scr, m_scr, l_scr, *, scale):
    has_prev = pl.program_id(2) > 0
    ri = lax.broadcasted_iota(jnp.int32, (QBLOCK, QBLOCK), 0)
    ci = lax.broadcasted_iota(jnp.int32, (QBLOCK, QBLOCK), 1)
    ri_first = ri + jnp.where(has_prev, 0, QBLOCK)

    def rows(n, r, dil):
        start = n * QBLOCK * dil + r
        return pl.ds(start, QBLOCK) if dil == 1 else pl.ds(start, QBLOCK, stride=dil)

    def bqk(a, b):
        return lax.dot_general(a, b, (((2,), (2,)), ((0,), (0,))), preferred_element_type=F32)

    def bpv(a, b):
        return lax.dot_general(a, b, (((2,), (1,)), ((0,), (0,))), preferred_element_type=F32)

    for bi, (window, dil) in enumerate(DIL_BRANCHES):
        assert window // dil == QBLOCK
        nblk = DIL_CHUNK // (QBLOCK * dil)
        units = [(n, r) for r in range(dil) for n in range(nblk)]
        loaded = {}
        for g in range(0, len(units), DIL_UNITS):
            group = units[g:g + DIL_UNITS]
            qs, kcs, vcs, kps, vps, thr = [], [], [], [], [], []
            for n, r in group:
                cur = rows(n, r, dil)
                qs.append(q_ref[0, cur, :].astype(BF16))
                loaded[n, r] = (kc_ref[0, cur, :].astype(BF16), vc_ref[0, cur, :].astype(BF16))
                kcs.append(loaded[n, r][0])
                vcs.append(loaded[n, r][1])
                if n > 0:
                    kp, vp = loaded.pop((n - 1, r))
                    thr.append(ri)
                else:
                    prv = rows(nblk - 1, r, dil)
                    kp, vp = kp_ref[0, prv, :].astype(BF16), vp_ref[0, prv, :].astype(BF16)
                    thr.append(ri_first)
                kps.append(kp)
                vps.append(vp)
            q = jnp.stack(qs)
            ci3 = jnp.broadcast_to(ci, (len(group), QBLOCK, QBLOCK))
            s_p = jnp.where(ci3 >= jnp.stack(thr), bqk(q, jnp.stack(kps)) * scale, -jnp.inf)
            s_c = jnp.where(ci3 <= jnp.broadcast_to(ri, ci3.shape), bqk(q, jnp.stack(kcs)) * scale, -jnp.inf)
            mx = jnp.max(jnp.maximum(s_p, s_c), axis=-1, keepdims=True)
            mx = jnp.broadcast_to(mx, ci3.shape)
            e_p = jnp.exp2(s_p - mx)
            e_c = jnp.exp2(s_c - mx)
            lsum = jnp.broadcast_to(jnp.sum(e_p + e_c, axis=-1, keepdims=True), ci3.shape)
            pv = bpv(e_p.astype(BF16), jnp.stack(vps)) + bpv(e_c.astype(BF16), jnp.stack(vcs))
            for u, (n, r) in enumerate(group):
                cur = rows(n, r, dil)
                acc_scr[bi, cur, :] = pv[u]
                m_scr[bi, cur, :] = mx[u]
                l_scr[bi, cur, :] = lsum[u]

    nb = len(DIL_BRANCHES)
    for c in range(DIL_CHUNK // DIL_MERGE_ROWS):
        rs = pl.ds(c * DIL_MERGE_ROWS, DIL_MERGE_ROWS)
        ms = [m_scr[b, rs, :] for b in range(nb)]
        top = functools.reduce(jnp.maximum, ms)
        ws = [jnp.exp2(m - top) for m in ms]
        num = sum(w * acc_scr[b, rs, :] for b, w in enumerate(ws))
        den = sum(w * l_scr[b, rs, :] for b, w in enumerate(ws))
        o_ref[0, rs, :] = (num / den).astype(o_ref.dtype)


def _dil_attention(qk, q_off, k_off, vbuf, v_off, batch, seq, heads):
    assert seq % DIL_CHUNK == 0
    qk3 = qk.reshape(batch, seq, -1)
    v3 = vbuf.reshape(batch, seq, -1)
    blk = (1, DIL_CHUNK, HEAD_DIM)
    cur = lambda off: pl.BlockSpec(blk, lambda b, h, c: (b, c, off + h))
    prev = lambda off: pl.BlockSpec(blk, lambda b, h, c: (b, jnp.maximum(c - 1, 0), off + h))
    return pl.pallas_call(
        functools.partial(_dil_kernel, scale=HEAD_DIM ** -0.5 * LOG2E),
        grid=(batch, heads, seq // DIL_CHUNK),
        in_specs=[cur(q_off), prev(k_off), cur(k_off), prev(v_off), cur(v_off)],
        out_specs=pl.BlockSpec(blk, lambda b, h, c: (b, c, h)),
        out_shape=jax.ShapeDtypeStruct((batch, seq, heads * HEAD_DIM), BF16),
        scratch_shapes=[pltpu.VMEM((len(DIL_BRANCHES), DIL_CHUNK, LANE), F32)] * 3,
        compiler_params=_cparams(("parallel", "parallel", "arbitrary")),
        name="dil_attn",
    )(qk3, qk3, qk3, v3, v3)


CAST_BLOCK_BYTES = 8 * 1024 * 1024


def _cast_kernel(w_ref, o_ref):
    o_ref[...] = w_ref[...].astype(o_ref.dtype)


def _to_bf16(w_stack, layer):
    _, rows, cols = w_stack.shape
    tr = next(t for t in (1024, 512, 256, 128, 64, 32, 16) if rows % t == 0 and t * cols * 4 <= CAST_BLOCK_BYTES)
    return pl.pallas_call(
        _cast_kernel,
        grid=(rows // tr,),
        in_specs=[pl.BlockSpec((None, tr, cols), lambda r: (layer, r, 0))],
        out_specs=pl.BlockSpec((tr, cols), lambda r: (r, 0)),
        out_shape=jax.ShapeDtypeStruct((rows, cols), BF16),
        compiler_params=_cparams(("parallel",)),
        name="w_bf16",
    )(w_stack)


def _rope_tables(seq):
    pos = jnp.arange(seq, dtype=jnp.int32).astype(F32)

    def cs(width):
        half = width // 2
        inv = ROPE_THETA ** (-jnp.arange(half, dtype=F32) / half)
        ang = pos[:, None] * inv[None, :]
        return jnp.cos(ang), jnp.sin(ang)

    cos, sin = cs(HEAD_DIM)
    full = (jnp.concatenate([cos, cos], axis=-1), jnp.concatenate([-sin, sin], axis=-1))
    cos, sin = cs(MLA_ROPE)
    z = jnp.zeros_like(cos)
    pad = jnp.zeros((seq, LANE - MLA_ROPE), F32)
    mla = (jnp.concatenate([cos, cos, pad], axis=-1),
           jnp.concatenate([-sin, z, pad], axis=-1),
           jnp.concatenate([z, sin, pad], axis=-1))
    return full, mla


def _layer(xf, xb, p_i, lam_init, alpha, tabs, dims, w_in, w_o, mla_q_norm, mla_w_uq, mla_kv_norm, mla_w_ukv,
           diff_lambda, diff_subln, ln_attn_g, ln_attn_b, w_ff1, w_ff2, ln_ff_g, ln_ff_b,
           w_ple_gate, w_ple_proj, ln_ple_g, ln_ple_b, last):
    batch, seq, d_model = dims
    rope_full, rope_mla = tabs
    mix_heads = d_model // HEAD_DIM
    sb_heads = dil_heads = mla_heads = mix_heads // 4
    diff_heads = mix_heads // 8
    q_rank, kv_rank = mla_q_norm.shape[0], mla_kv_norm.shape[0]
    hw = sb_heads * HEAD_DIM
    widths = (hw,) * 3 + (hw,) * 3 + (q_rank, kv_rank, MLA_ROPE) + (diff_heads * 2 * HEAD_DIM,) * 3
    offs = [0]
    for wd in widths:
        offs.append(offs[-1] + wd)
    w_in = _to_bf16(*w_in)
    col = lambda n: w_in[:, offs[n]:offs[n + 1]]
    a_q, a_k, a_v, b_q, b_k, b_v, c_q, c_kv, c_kr, d_q, d_k, d_v = (col(n) for n in range(12))

    w_plain = jnp.concatenate([a_q, a_k, a_v, d_v], axis=1)
    w_rope_f = jnp.concatenate([b_q, b_k], axis=1)
    w_rope_h = jnp.concatenate([d_q, d_k], axis=1)
    c_used = q_rank + kv_rank + LANE
    c_width = -(-c_used // 1024) * 1024
    w_c = jnp.concatenate([c_q, c_kv, c_kr, jnp.zeros((d_model, c_width - c_used + LANE - MLA_ROPE), BF16),
                           b_v], axis=1)

    h_plain = _mm(xb, w_plain, BF16)
    h_rope_f = _mm(xb, w_rope_f, F32, rope=rope_full, seq=seq)
    h_rope_h = _mm(xb, w_rope_h, BF16, rope=rope_full, seq=seq)
    h_c = _mm(xb, w_c, F32)

    y_a = _sb_attention(h_plain, batch, seq, sb_heads, 0, sb_heads, 2 * sb_heads)
    y_b = _dil_attention(h_rope_f, 0, dil_heads, h_c, c_width // HEAD_DIM, batch, seq, dil_heads)

    wq = mla_w_uq.reshape(q_rank, mla_heads, MLA_NOPE + MLA_ROPE)
    wq = jnp.pad(wq, ((0, 0), (0, 0), (0, MLA_QK_PAD - MLA_NOPE - MLA_ROPE))).reshape(q_rank, -1).astype(BF16)
    wkv = mla_w_ukv.reshape(kv_rank, mla_heads, MLA_NOPE + MLA_V)
    wk = wkv[:, :, :MLA_NOPE].reshape(kv_rank, -1).astype(BF16)
    wv = wkv[:, :, MLA_NOPE:].reshape(kv_rank, -1).astype(BF16)
    mq, mk, mv = _mla_up(h_c, c_width, mla_q_norm, mla_kv_norm, wq, wk, wv, rope_mla, seq, mla_heads)
    y_c = _mla_attention(mq, mk, mv, batch, seq, mla_heads)

    y_d = _diff_attention(h_rope_h, 0, diff_heads, h_plain, 3 * sb_heads // 2, diff_lambda, diff_subln,
                          lam_init, batch, seq, diff_heads)

    mix = jnp.concatenate([y_a, y_b, y_c, y_d], axis=-1).reshape(batch * seq, -1)
    xf, xb = _mm_ln(mix, _to_bf16(*w_o), xf, ln_attn_g, ln_attn_b, alpha)

    u = _mm_ws_relu2(xb, *w_ff1)
    xf, xb = _mm_ln(u, _to_bf16(*w_ff2), xf, ln_ff_g, ln_ff_b, alpha)

    xf, xb = _mm_ln(xb, _to_bf16(*w_ple_gate), xf, ln_ple_g, ln_ple_b, alpha,
                    ple=(p_i.astype(BF16), w_ple_proj.astype(BF16)), emit_bf16=not last)
    return xf, xb


def kernel(x, p, w_in, w_o, mla_q_norm, mla_w_uq, mla_kv_norm, mla_w_ukv, diff_lambda, diff_subln,
           ln_attn_g, ln_attn_b, w_ff1, w_ff2, ln_ff_g, ln_ff_b, w_ple_gate, w_ple_proj, ln_ple_g, ln_ple_b):
    batch, seq, d_model = x.shape
    depth = w_in.shape[0]
    alpha = (2 * depth) ** 0.25
    tabs = _rope_tables(seq)
    xf = x.reshape(batch * seq, d_model)
    xb = xf.astype(BF16)
    for i in range(depth):
        lam_init = 0.8 - 0.6 * math.exp(-0.3 * i)
        xf, xb = _layer(xf, xb, p[i].reshape(batch * seq, -1), lam_init, alpha, tabs, (batch, seq, d_model),
                        (w_in, i), (w_o, i), mla_q_norm[i], mla_w_uq[i], mla_kv_norm[i], mla_w_ukv[i],
                        diff_lambda[i], diff_subln[i], ln_attn_g[i], ln_attn_b[i], (w_ff1, i), (w_ff2, i),
                        ln_ff_g[i], ln_ff_b[i], (w_ple_gate, i), w_ple_proj[i], ln_ple_g[i], ln_ple_b[i],
                        last=i == depth - 1)
    return xf.reshape(batch, seq, d_model)
```

```python
import functools
import math

import jax
import jax.numpy as jnp
from jax import lax
from jax.experimental import pallas as pl
from jax.experimental.pallas import tpu as pltpu

F32 = jnp.float32
BF16 = jnp.bfloat16

LANE = 128
HEAD_DIM = 128
QBLOCK = 128
ROPE_THETA = 10000.0
LN_EPS = 1e-5
RMS_EPS = 1e-6
MLA_NOPE = 128
MLA_ROPE = 64
MLA_V = 128
MLA_QK_PAD = 256
DIL_BRANCHES = ((128, 1), (512, 4), (2048, 16))
DIL_CHUNK = QBLOCK * 16
VMEM_LIMIT_BYTES = 56 * 1024 * 1024
SB_EXP_UNDERFLOW = -104.0
SB_SUB = 4
LOG2E = 1.4426950408889634
FLASH_TQ = 1024
FLASH_TK = 512
DIL_MERGE_ROWS = 256
DIL_UNITS = 16


def _pick(n, cands):
    for c in cands:
        if n % c == 0:
            return c
    return n


def _cparams(sem):
    return pltpu.CompilerParams(dimension_semantics=sem, vmem_limit_bytes=VMEM_LIMIT_BYTES)


def _dot(a, b):
    return jnp.dot(a, b, preferred_element_type=F32)


def _dot_nt(a, b):
    return lax.dot_general(a, b, (((1,), (1,)), ((), ())), preferred_element_type=F32)


def _mm_kernel(*refs, rope):
    if rope:
        x_ref, w_ref, cos_ref, sin_ref, o_ref = refs
    else:
        x_ref, w_ref, o_ref = refs
    acc = _dot(x_ref[...], w_ref[...])
    if rope:
        cos = cos_ref[...]
        sin = sin_ref[...]
        for c in range(acc.shape[1] // LANE):
            blk = acc[:, c * LANE:(c + 1) * LANE]
            rot = pltpu.roll(blk, LANE // 2, 1)
            o_ref[:, c * LANE:(c + 1) * LANE] = (blk * cos + rot * sin).astype(o_ref.dtype)
    else:
        o_ref[...] = acc.astype(o_ref.dtype)


def _mm(x, w, out_dtype, *, rope=None, seq=None):
    M, K = x.shape
    N = w.shape[1]
    tm = _pick(M, (1024, 512, 256, 128))
    tn = _pick(N, (1024, 512, 256, 128))
    in_specs = [pl.BlockSpec((tm, K), lambda i, j: (i, 0)),
                pl.BlockSpec((K, tn), lambda i, j: (0, j))]
    args = [x, w]
    if rope is not None:
        tm = _pick(seq, (tm, 512, 256, 128))
        in_specs[0] = pl.BlockSpec((tm, K), lambda i, j: (i, 0))
        nsb = seq // tm
        tab = pl.BlockSpec((tm, LANE), lambda i, j: (i % nsb, 0))
        in_specs += [tab, tab]
        args += list(rope)
    return pl.pallas_call(
        functools.partial(_mm_kernel, rope=rope is not None),
        grid=(M // tm, N // tn),
        in_specs=in_specs,
        out_specs=pl.BlockSpec((tm, tn), lambda i, j: (i, j)),
        out_shape=jax.ShapeDtypeStruct((M, N), out_dtype),
        compiler_params=_cparams(("parallel", "parallel")),
        name="proj_mm",
    )(*args)


WS_CAST_ROWS = 512


def _mm_ws_kernel(x_ref, w_ref, o_ref, wb_ref):
    @pl.when(pl.program_id(1) == 0)
    def _():
        for r in range(0, w_ref.shape[0], WS_CAST_ROWS):
            wb_ref[r:r + WS_CAST_ROWS, :] = w_ref[r:r + WS_CAST_ROWS, :].astype(BF16)

    acc = _dot(x_ref[...], wb_ref[...])
    o_ref[...] = jnp.square(jnp.maximum(acc, 0.0)).astype(o_ref.dtype)


def _mm_ws_relu2(x, w_stack, layer):
    M, K = x.shape
    N = w_stack.shape[2]
    tm = _pick(M, (512, 256, 128))
    tn = _pick(N, (1024, 512, 256, 128))
    assert K % WS_CAST_ROWS == 0
    return pl.pallas_call(
        _mm_ws_kernel,
        grid=(N // tn, M // tm),
        in_specs=[pl.BlockSpec((tm, K), lambda j, i: (i, 0)),
                  pl.BlockSpec((None, K, tn), lambda j, i: (layer, 0, j))],
        out_specs=pl.BlockSpec((tm, tn), lambda j, i: (i, j)),
        out_shape=jax.ShapeDtypeStruct((M, N), BF16),
        scratch_shapes=[pltpu.VMEM((K, tn), BF16)],
        compiler_params=_cparams(("arbitrary", "arbitrary")),
        name="ff1_mm",
    )(x, w_stack)


LN_ROWS = 32
LN_CHAINS = 2
LN_COLS = 1024


def _mm_ln_kernel(*refs, n_i, alpha, ple, emit_bf16):
    refs = list(refs)
    x_ref, w_ref, r_ref, g_ref, b_ref = refs[:5]
    rest = refs[5:]
    if ple:
        p_ref, wp_ref = rest[:2]
        rest = rest[2:]
    o_ref = rest[0]
    o16_ref = rest[1] if emit_bf16 else None
    accs = rest[-2:]
    i = pl.program_id(0)
    k = pl.program_id(1)
    n_total = accs[0].shape[1]
    nc = _pick(n_total, (LN_COLS, LANE))
    rc = o_ref.shape[0]

    def accumulate(acc_ref):
        x = x_ref[...]
        for c in range(n_total // nc):
            cols = slice(c * nc, (c + 1) * nc)
            acc_ref[:, cols] += _dot(x, w_ref[:, cols])

    def normalise(acc_ref):
        g = g_ref[...]
        b = b_ref[...]
        base = pl.multiple_of(k * rc, rc)
        if ple:
            proj = _dot(p_ref[...], wp_ref[...])

        def norm_rows(lo):
            rows = pl.ds(base + lo, LN_ROWS)
            f = acc_ref[rows, :]
            acc_ref[rows, :] = jnp.zeros((LN_ROWS, n_total), F32)
            if ple:
                f = (0.5 * jnp.tanh(0.5 * f) + 0.5) * proj[lo:lo + LN_ROWS, :]
            y = alpha * r_ref[lo:lo + LN_ROWS, :] + f
            mu = jnp.mean(y, axis=-1, keepdims=True)
            d = y - mu
            var = jnp.mean(d * d, axis=-1, keepdims=True)
            return d * lax.rsqrt(var + LN_EPS) * g + b

        for lo0 in range(0, rc, LN_ROWS * LN_CHAINS):
            los = [lo0 + u * LN_ROWS for u in range(LN_CHAINS) if lo0 + u * LN_ROWS < rc]
            outs = [norm_rows(lo) for lo in los]
            for lo, out in zip(los, outs):
                o_ref[lo:lo + LN_ROWS, :] = out
                if emit_bf16:
                    o16_ref[lo:lo + LN_ROWS, :] = out.astype(BF16)

    @pl.when(i == 0)
    def _():
        @pl.when(k == 0)
        def _():
            for acc_ref in accs:
                acc_ref[...] = jnp.zeros_like(acc_ref)

        accumulate(accs[0])
        o_ref[...] = jnp.zeros_like(o_ref)
        if emit_bf16:
            o16_ref[...] = jnp.zeros_like(o16_ref)

    for par in range(2):
        @pl.when(jnp.logical_and(jnp.logical_and(i > 0, i < n_i), i % 2 == par))
        def _():
            normalise(accs[1 - par])
            accumulate(accs[par])

    @pl.when(i == n_i)
    def _():
        normalise(accs[(n_i - 1) % 2])


def _mm_ln(x, w, resid, g, b, alpha, *, ple=None, emit_bf16=True):
    M, K = x.shape
    N = w.shape[1]
    tm = _pick(M, (512, 256, 128))
    tk = _pick(K, (1024, 512, 256, 128))
    n_i, nk = M // tm, K // tk
    rc = tm // nk
    assert rc % LN_ROWS == 0
    kk = lambda i, k: jnp.where(i == n_i, nk - 1, k)
    chunk = lambda i, k: (jnp.maximum((i - 1) * nk + k, 0), 0)
    const = lambda i, k: (0, 0)
    in_specs = [pl.BlockSpec((tm, tk), lambda i, k: (jnp.minimum(i, n_i - 1), kk(i, k))),
                pl.BlockSpec((tk, N), lambda i, k: (kk(i, k), 0)),
                pl.BlockSpec((rc, N), chunk),
                pl.BlockSpec((1, N), const),
                pl.BlockSpec((1, N), const)]
    args = [x, w, resid, g.reshape(1, N), b.reshape(1, N)]
    if ple is not None:
        pp, wp = ple
        in_specs += [pl.BlockSpec((rc, pp.shape[1]), chunk), pl.BlockSpec(wp.shape, const)]
        args += [pp, wp]
    out_specs = [pl.BlockSpec((rc, N), chunk)]
    out_shape = [jax.ShapeDtypeStruct((M, N), F32)]
    if emit_bf16:
        out_specs.append(pl.BlockSpec((rc, N), chunk))
        out_shape.append(jax.ShapeDtypeStruct((M, N), BF16))
    outs = pl.pallas_call(
        functools.partial(_mm_ln_kernel, n_i=n_i, alpha=alpha, ple=ple is not None, emit_bf16=emit_bf16),
        grid=(n_i + 1, nk),
        in_specs=in_specs,
        out_specs=out_specs,
        out_shape=out_shape,
        scratch_shapes=[pltpu.VMEM((tm, N), F32)] * 2,
        compiler_params=_cparams(("arbitrary", "arbitrary")),
        name="mm_res_ln",
    )(*args)
    return (outs[0], outs[1]) if emit_bf16 else (outs[0], None)


def _rope64(xr, cos, s_lo, s_hi):
    return xr * cos + pltpu.roll(xr, LANE - MLA_ROPE // 2, 1) * s_lo + pltpu.roll(xr, MLA_ROPE // 2, 1) * s_hi


def _mla_up_kernel(c_ref, gq_ref, gkv_ref, wq_ref, wk_ref, wv_ref, cos_ref, slo_ref, shi_ref,
                   q_ref, k_ref, v_ref, *, q_rank, kv_rank, heads):
    cos = cos_ref[...]
    s_lo = slo_ref[...]
    s_hi = shi_ref[...]

    def rms(t, g):
        return (t * lax.rsqrt(jnp.mean(t * t, axis=-1, keepdims=True) + RMS_EPS) * g).astype(BF16)

    nq = rms(c_ref[:, :q_rank], gq_ref[...])
    qf = _dot(nq, wq_ref[...])
    for h in range(heads):
        lo = h * MLA_QK_PAD
        q_ref[:, lo:lo + MLA_NOPE] = qf[:, lo:lo + MLA_NOPE].astype(BF16)
        q_ref[:, lo + MLA_NOPE:lo + MLA_QK_PAD] = _rope64(
            qf[:, lo + MLA_NOPE:lo + MLA_QK_PAD], cos, s_lo, s_hi).astype(BF16)
    nkv = rms(c_ref[:, q_rank:q_rank + kv_rank], gkv_ref[...])
    kf = _dot(nkv, wk_ref[...])
    v_ref[...] = _dot(nkv, wv_ref[...]).astype(BF16)
    kr = _rope64(c_ref[:, q_rank + kv_rank:q_rank + kv_rank + LANE], cos, s_lo, s_hi).astype(BF16)
    for h in range(heads):
        lo = h * MLA_QK_PAD
        k_ref[:, lo:lo + MLA_NOPE] = kf[:, h * MLA_NOPE:(h + 1) * MLA_NOPE].astype(BF16)
        k_ref[:, lo + MLA_NOPE:lo + MLA_QK_PAD] = kr


def _mla_up(cbuf, c_width, gq, gkv, wq, wk, wv, tabs, seq, heads):
    M = cbuf.shape[0]
    q_rank, kv_rank = gq.shape[0], gkv.shape[0]
    tm = _pick(seq, (512, 256, 128))
    nsb = seq // tm
    const = lambda shape: pl.BlockSpec(shape, lambda i: (0, 0))
    tab = pl.BlockSpec((tm, LANE), lambda i: (i % nsb, 0))
    row = lambda n: pl.BlockSpec((tm, n), lambda i: (i, 0))
    return pl.pallas_call(
        functools.partial(_mla_up_kernel, q_rank=q_rank, kv_rank=kv_rank, heads=heads),
        grid=(M // tm,),
        in_specs=[row(c_width), const((1, q_rank)), const((1, kv_rank)),
                  const(wq.shape), const(wk.shape), const(wv.shape), tab, tab, tab],
        out_specs=[row(heads * MLA_QK_PAD), row(heads * MLA_QK_PAD), row(heads * MLA_V)],
        out_shape=[jax.ShapeDtypeStruct((M, heads * MLA_QK_PAD), BF16),
                   jax.ShapeDtypeStruct((M, heads * MLA_QK_PAD), BF16),
                   jax.ShapeDtypeStruct((M, heads * MLA_V), BF16)],
        compiler_params=_cparams(("parallel",)),
        name="mla_up",
    )(cbuf, gq.reshape(1, -1), gkv.reshape(1, -1), wq, wk, wv, *tabs)


def _attn_specs(tq, seq, dq, dk, dv, q_off, k_off, v_off):
    return [pl.BlockSpec((1, tq, dq), lambda b, h, i: (b, i, q_off + h)),
            pl.BlockSpec((1, seq, dk), lambda b, h, i: (b, 0, k_off + h)),
            pl.BlockSpec((1, seq, dv), lambda b, h, i: (b, 0, v_off + h))]


def _sb_kernel(q_ref, k_ref, v_ref, tri_ref, o_ref, *, t, scale):
    step = pl.program_id(2)
    tri = tri_ref[...]
    row = lax.broadcasted_iota(jnp.int32, (t, t), 0)
    col = lax.broadcasted_iota(jnp.int32, (t, t), 1)
    past = col < row

    def block(q, j, c, acc, keep):
        start = pl.multiple_of(j * t, t)
        ks = k_ref[0, pl.ds(start, t), :]
        vs = v_ref[0, pl.ds(start, t), :]
        z = _dot_nt(q, ks) * scale
        log_beta = jnp.minimum(z, 0.0) - jnp.log1p(jnp.exp(-jnp.abs(z)))
        log_1m = log_beta - z
        if keep is not None:
            log_1m = jnp.where(keep, log_1m, 0.0)
        hi = log_1m.astype(BF16)
        lo = (log_1m - hi.astype(F32)).astype(BF16)
        between = _dot(hi, tri) + _dot(lo, tri) + c
        w = jnp.exp(log_beta + between)
        if keep is not None:
            w = jnp.where(keep, w, 0.0)
        acc = acc + _dot(w.astype(BF16), vs)
        c = c + jnp.sum(log_1m, axis=-1, keepdims=True)
        return c, acc

    nsub = q_ref.shape[1] // t
    qs = [q_ref[0, u * t:(u + 1) * t, :] for u in range(nsub)]
    idx = [step * nsub + u for u in range(nsub)]
    state = []
    for q, i in zip(qs, idx):
        c, acc = block(q, i, jnp.zeros((t, 1), F32), jnp.zeros((t, v_ref.shape[2]), F32), past)
        state.append(block(q, jnp.maximum(i - 1, 0), c, acc, row < jnp.where(i > 0, t, 0)))

    for u, (q, i, (c, acc)) in enumerate(zip(qs, idx, state)):
        def cond(carry, i=i):
            jj, c_max, _, _ = carry
            return jnp.logical_and(jj < i, c_max > SB_EXP_UNDERFLOW)

        def body(carry, q=q, i=i):
            jj, _, c, acc = carry
            c, acc = block(q, i - 1 - jj, c, acc, None)
            return jj + 1, jnp.max(c), c, acc

        _, _, c, acc = lax.while_loop(cond, body, (jnp.int32(1), jnp.max(c), c, acc))
        o_ref[0, u * t:(u + 1) * t, :] = acc.astype(o_ref.dtype)


def _sb_attention(qkv, batch, seq, heads, q_off, k_off, v_off):
    t = _pick(seq, (256, 128))
    tq = _pick(seq, (SB_SUB * t, t))
    arr = qkv.reshape(batch, seq, -1)
    tri = (lax.broadcasted_iota(jnp.int32, (t, t), 0) > lax.broadcasted_iota(jnp.int32, (t, t), 1)).astype(BF16)
    specs = _attn_specs(tq, seq, HEAD_DIM, HEAD_DIM, HEAD_DIM, q_off, k_off, v_off)
    specs.append(pl.BlockSpec((t, t), lambda b, h, i: (0, 0)))
    return pl.pallas_call(
        functools.partial(_sb_kernel, t=t, scale=HEAD_DIM ** -0.5),
        grid=(batch, heads, seq // tq),
        in_specs=specs,
        out_specs=pl.BlockSpec((1, tq, HEAD_DIM), lambda b, h, i: (b, i, h)),
        out_shape=jax.ShapeDtypeStruct((batch, seq, heads * HEAD_DIM), BF16),
        compiler_params=_cparams(("parallel", "parallel", "arbitrary")),
        name="sb_attn",
    )(arr, arr, arr, tri)


def _softmax_pv(s, vs, m, l, acc, mask):
    if mask is not None:
        s = jnp.where(mask, s, -jnp.inf)
    m_new = jnp.maximum(m, jnp.max(s, axis=-1, keepdims=True))
    a = jnp.exp2(m - m_new)
    p = jnp.exp2(s - m_new)
    l = a * l + jnp.sum(p, axis=-1, keepdims=True)
    acc = a * acc + _dot(p.astype(BF16), vs)
    return m_new, l, acc


def _flash_init(t, dv):
    return jnp.full((t, 1), -jnp.inf, F32), jnp.zeros((t, 1), F32), jnp.zeros((t, dv), F32)


def _causal_sweep(i, tq, tk, step, init):
    nsub = tq // tk
    row = lax.broadcasted_iota(jnp.int32, (tq, tk), 0)
    col = lax.broadcasted_iota(jnp.int32, (tq, tk), 1)
    def body(jj, c):
        for u in range(nsub):
            c = step(jj * nsub + u, c, None)
        return c

    carry = lax.fori_loop(0, i, body, init)
    for u in range(nsub):
        carry = step(i * nsub + u, carry, col + u * tk <= row)
    return carry


def _flash_tiles(seq):
    tq = _pick(seq, (FLASH_TQ, 512, 256, 128))
    return tq, _pick(tq, (FLASH_TK, 256, 128))


def _mla_kernel(q_ref, k_ref, v_ref, o_ref, *, tq, tk, scale):
    q = q_ref[0]

    def step(j, carry, mask):
        rows = pl.ds(pl.multiple_of(j * tk, tk), tk)
        s = _dot_nt(q, k_ref[0, rows, :]) * scale
        return _softmax_pv(s, v_ref[0, rows, :], *carry, mask)

    m, l, acc = _causal_sweep(pl.program_id(2), tq, tk, step, _flash_init(tq, v_ref.shape[2]))
    o_ref[0] = (acc / l).astype(o_ref.dtype)


def _mla_attention(q, k, v, batch, seq, heads):
    t, tk = _flash_tiles(seq)
    return pl.pallas_call(
        functools.partial(_mla_kernel, tq=t, tk=tk, scale=(MLA_NOPE + MLA_ROPE) ** -0.5 * LOG2E),
        grid=(batch, heads, seq // t),
        in_specs=_attn_specs(t, seq, MLA_QK_PAD, MLA_QK_PAD, MLA_V, 0, 0, 0),
        out_specs=pl.BlockSpec((1, t, MLA_V), lambda b, h, i: (b, i, h)),
        out_shape=jax.ShapeDtypeStruct((batch, seq, heads * MLA_V), BF16),
        compiler_params=_cparams(("parallel", "parallel", "arbitrary")),
        name="mla_attn",
    )(q.reshape(batch, seq, -1), k.reshape(batch, seq, -1), v.reshape(batch, seq, -1))


def _diff_kernel(q_ref, k_ref, v_ref, lam_ref, g_ref, o_ref, *, tq, tk, scale, lam_init):
    d = HEAD_DIM
    q1 = q_ref[0, :, :d]
    q2 = q_ref[0, :, d:]

    def step(j, carry, mask):
        rows = pl.ds(pl.multiple_of(j * tk, tk), tk)
        vs = v_ref[0, rows, :]
        s1 = _dot_nt(q1, k_ref[0, rows, :d]) * scale
        s2 = _dot_nt(q2, k_ref[0, rows, d:]) * scale
        return _softmax_pv(s1, vs, *carry[:3], mask) + _softmax_pv(s2, vs, *carry[3:], mask)

    one = _flash_init(tq, v_ref.shape[2])
    m1, l1, a1, m2, l2, a2 = _causal_sweep(pl.program_id(2), tq, tk, step, one + one)
    lp = lam_ref[...]
    lam = (jnp.exp(jnp.sum(lp[0:1] * lp[1:2], axis=-1, keepdims=True))
           - jnp.exp(jnp.sum(lp[2:3] * lp[3:4], axis=-1, keepdims=True)) + lam_init)
    o = a1 / l1 - lam * (a2 / l2)
    o = o * lax.rsqrt(jnp.mean(o * o, axis=-1, keepdims=True) + RMS_EPS) * g_ref[...]
    o_ref[0] = (o * (1.0 - lam_init)).astype(o_ref.dtype)


def _diff_attention(qk, q_off, k_off, vbuf, v_off, lam_params, subln_g, lam_init, batch, seq, heads):
    t, tk = _flash_tiles(seq)
    dd = 2 * HEAD_DIM
    specs = [pl.BlockSpec((1, t, dd), lambda b, h, i: (b, i, q_off + h)),
             pl.BlockSpec((1, seq, dd), lambda b, h, i: (b, 0, k_off + h)),
             pl.BlockSpec((1, seq, dd), lambda b, h, i: (b, 0, v_off + h)),
             pl.BlockSpec(lam_params.shape, lambda b, h, i: (0, 0)),
             pl.BlockSpec((1, dd), lambda b, h, i: (0, 0))]
    return pl.pallas_call(
        functools.partial(_diff_kernel, tq=t, tk=tk, scale=HEAD_DIM ** -0.5 * LOG2E, lam_init=lam_init),
        grid=(batch, heads, seq // t),
        in_specs=specs,
        out_specs=pl.BlockSpec((1, t, dd), lambda b, h, i: (b, i, h)),
        out_shape=jax.ShapeDtypeStruct((batch, seq, heads * dd), BF16),
        compiler_params=_cparams(("parallel", "parallel", "arbitrary")),
        name="diff_attn",
    )(qk.reshape(batch, seq, -1), qk.reshape(batch, seq, -1), vbuf.reshape(batch, seq, -1),
      lam_params, subln_g.reshape(1, dd))


def _dil_kernel(q_ref, kp_ref, kc_ref, vp_ref, vc_ref, o_ref, acc_scr, m_scr, l_scr, *, scale):
    has_prev = pl.program_id(2) > 0
    ri = lax.broadcasted_iota(jnp.int32, (QBLOCK, QBLOCK), 0)
    ci = lax.broadcasted_iota(jnp.int32, (QBLOCK, QBLOCK), 1)
    ri_first = ri + jnp.where(has_prev, 0, QBLOCK)

    def rows(n, r, dil):
        start = n * QBLOCK * dil + r
        return pl.ds(start, QBLOCK) if dil == 1 else pl.ds(start, QBLOCK, stride=dil)

    def bqk(a, b):
        return lax.dot_general(a, b, (((2,), (2,)), ((0,), (0,))), preferred_element_type=F32)

    def bpv(a, b):
        return lax.dot_general(a, b, (((2,), (1,)), ((0,), (0,))), preferred_element_type=F32)

    for bi, (window, dil) in enumerate(DIL_BRANCHES):
        assert window // dil == QBLOCK
        nblk = DIL_CHUNK // (QBLOCK * dil)
        units = [(n, r) for r in range(dil) for n in range(nblk)]
        loaded = {}
        for g in range(0, len(units), DIL_UNITS):
            group = units[g:g + DIL_UNITS]
            qs, kcs, vcs, kps, vps, thr = [], [], [], [], [], []
            for n, r in group:
                cur = rows(n, r, dil)
                qs.append(q_ref[0, cur, :].astype(BF16))
                loaded[n, r] = (kc_ref[0, cur, :].astype(BF16), vc_ref[0, cur, :].astype(BF16))
                kcs.append(loaded[n, r][0])
                vcs.append(loaded[n, r][1])
                if n > 0:
                    kp, vp = loaded.pop((n - 1, r))
                    thr.append(ri)
                else:
                    prv = rows(nblk - 1, r, dil)
                    kp, vp = kp_ref[0, prv, :].astype(BF16), vp_ref[0, prv, :].astype(BF16)
                    thr.append(ri_first)
                kps.append(kp)
                vps.append(vp)
            q = jnp.stack(qs)
            ci3 = jnp.broadcast_to(ci, (len(group), QBLOCK, QBLOCK))
            s_p = jnp.where(ci3 >= jnp.stack(thr), bqk(q, jnp.stack(kps)) * scale, -jnp.inf)
            s_c = jnp.where(ci3 <= jnp.broadcast_to(ri, ci3.shape), bqk(q, jnp.stack(kcs)) * scale, -jnp.inf)
            mx = jnp.max(jnp.maximum(s_p, s_c), axis=-1, keepdims=True)
            mx = jnp.broadcast_to(mx, ci3.shape)
            e_p = jnp.exp2(s_p - mx)
            e_c = jnp.exp2(s_c - mx)
            lsum = jnp.broadcast_to(jnp.sum(e_p + e_c, axis=-1, keepdims=True), ci3.shape)
            pv = bpv(e_p.astype(BF16), jnp.stack(vps)) + bpv(e_c.astype(BF16), jnp.stack(vcs))
            for u, (n, r) in enumerate(group):
                cur = rows(n, r, dil)
                acc_scr[bi, cur, :] = pv[u]
                m_scr[bi, cur, :] = mx[u]
                l_scr[bi, cur, :] = lsum[u]

    nb = len(DIL_BRANCHES)
    for c in range(DIL_CHUNK // DIL_MERGE_ROWS):
        rs = pl.ds(c * DIL_MERGE_ROWS, DIL_MERGE_ROWS)
        ms = [m_scr[b, rs, :] for b in range(nb)]
        top = functools.reduce(jnp.maximum, ms)
        ws = [jnp.exp2(m - top) for m in ms]
        num = sum(w * acc_scr[b, rs, :] for b, w in enumerate(ws))
        den = sum(w * l_scr[b, rs, :] for b, w in enumerate(ws))
        o_ref[0, rs, :] = (num / den).astype(o_ref.dtype)


def _dil_attention(qk, q_off, k_off, vbuf, v_off, batch, seq, heads):
    assert seq % DIL_CHUNK == 0
    qk3 = qk.reshape(batch, seq, -1)
    v3 = vbuf.reshape(batch, seq, -1)
    blk = (1, DIL_CHUNK, HEAD_DIM)
    cur = lambda off: pl.BlockSpec(blk, lambda b, h, c: (b, c, off + h))
    prev = lambda off: pl.BlockSpec(blk, lambda b, h, c: (b, jnp.maximum(c - 1, 0), off + h))
    return pl.pallas_call(
        functools.partial(_dil_kernel, scale=HEAD_DIM ** -0.5 * LOG2E),
        grid=(batch, heads, seq // DIL_CHUNK),
        in_specs=[cur(q_off), prev(k_off), cur(k_off), prev(v_off), cur(v_off)],
        out_specs=pl.BlockSpec(blk, lambda b, h, c: (b, c, h)),
        out_shape=jax.ShapeDtypeStruct((batch, seq, heads * HEAD_DIM), BF16),
        scratch_shapes=[pltpu.VMEM((len(DIL_BRANCHES), DIL_CHUNK, LANE), F32)] * 3,
        compiler_params=_cparams(("parallel", "parallel", "arbitrary")),
        name="dil_attn",
    )(qk3, qk3, qk3, v3, v3)


CAST_BLOCK_BYTES = 8 * 1024 * 1024


def _cast_kernel(w_ref, o_ref):
    o_ref[...] = w_ref[...].astype(o_ref.dtype)


def _to_bf16(w_stack, layer):
    _, rows, cols = w_stack.shape
    tr = next(t for t in (1024, 512, 256, 128, 64, 32, 16) if rows % t == 0 and t * cols * 4 <= CAST_BLOCK_BYTES)
    return pl.pallas_call(
        _cast_kernel,
        grid=(rows // tr,),
        in_specs=[pl.BlockSpec((None, tr, cols), lambda r: (layer, r, 0))],
        out_specs=pl.BlockSpec((tr, cols), lambda r: (r, 0)),
        out_shape=jax.ShapeDtypeStruct((rows, cols), BF16),
        compiler_params=_cparams(("parallel",)),
        name="w_bf16",
    )(w_stack)


def _rope_tables(seq):
    pos = jnp.arange(seq, dtype=jnp.int32).astype(F32)

    def cs(width):
        half = width // 2
        inv = ROPE_THETA ** (-jnp.arange(half, dtype=F32) / half)
        ang = pos[:, None] * inv[None, :]
        return jnp.cos(ang), jnp.sin(ang)

    cos, sin = cs(HEAD_DIM)
    full = (jnp.concatenate([cos, cos], axis=-1), jnp.concatenate([-sin, sin], axis=-1))
    cos, sin = cs(MLA_ROPE)
    z = jnp.zeros_like(cos)
    pad = jnp.zeros((seq, LANE - MLA_ROPE), F32)
    mla = (jnp.concatenate([cos, cos, pad], axis=-1),
           jnp.concatenate([-sin, z, pad], axis=-1),
           jnp.concatenate([z, sin, pad], axis=-1))
    return full, mla


def _layer(xf, xb, p_i, lam_init, alpha, tabs, dims, w_in, w_o, mla_q_norm, mla_w_uq, mla_kv_norm, mla_w_ukv,
           diff_lambda, diff_subln, ln_attn_g, ln_attn_b, w_ff1, w_ff2, ln_ff_g, ln_ff_b,
           w_ple_gate, w_ple_proj, ln_ple_g, ln_ple_b, last):
    batch, seq, d_model = dims
    rope_full, rope_mla = tabs
    mix_heads = d_model // HEAD_DIM
    sb_heads = dil_heads = mla_heads = mix_heads // 4
    diff_heads = mix_heads // 8
    q_rank, kv_rank = mla_q_norm.shape[0], mla_kv_norm.shape[0]
    hw = sb_heads * HEAD_DIM
    widths = (hw,) * 3 + (hw,) * 3 + (q_rank, kv_rank, MLA_ROPE) + (diff_heads * 2 * HEAD_DIM,) * 3
    offs = [0]
    for wd in widths:
        offs.append(offs[-1] + wd)
    w_in = lax.optimization_barrier(w_in.astype(BF16))
    col = lambda n: w_in[:, offs[n]:offs[n + 1]]
    a_q, a_k, a_v, b_q, b_k, b_v, c_q, c_kv, c_kr, d_q, d_k, d_v = (col(n) for n in range(12))

    w_plain = jnp.concatenate([a_q, a_k, a_v, d_v], axis=1)
    w_rope_f = jnp.concatenate([b_q, b_k], axis=1)
    w_rope_h = jnp.concatenate([d_q, d_k], axis=1)
    c_used = q_rank + kv_rank + LANE
    c_width = -(-c_used // 1024) * 1024
    w_c = jnp.concatenate([c_q, c_kv, c_kr, jnp.zeros((d_model, c_width - c_used + LANE - MLA_ROPE), BF16),
                           b_v], axis=1)

    h_plain = _mm(xb, w_plain, BF16)
    h_rope_f = _mm(xb, w_rope_f, F32, rope=rope_full, seq=seq)
    h_rope_h = _mm(xb, w_rope_h, BF16, rope=rope_full, seq=seq)
    h_c = _mm(xb, w_c, F32)

    y_a = _sb_attention(h_plain, batch, seq, sb_heads, 0, sb_heads, 2 * sb_heads)
    y_b = _dil_attention(h_rope_f, 0, dil_heads, h_c, c_width // HEAD_DIM, batch, seq, dil_heads)

    wq = mla_w_uq.reshape(q_rank, mla_heads, MLA_NOPE + MLA_ROPE)
    wq = jnp.pad(wq, ((0, 0), (0, 0), (0, MLA_QK_PAD - MLA_NOPE - MLA_ROPE))).reshape(q_rank, -1).astype(BF16)
    wkv = mla_w_ukv.reshape(kv_rank, mla_heads, MLA_NOPE + MLA_V)
    wk = wkv[:, :, :MLA_NOPE].reshape(kv_rank, -1).astype(BF16)
    wv = wkv[:, :, MLA_NOPE:].reshape(kv_rank, -1).astype(BF16)
    mq, mk, mv = _mla_up(h_c, c_width, mla_q_norm, mla_kv_norm, wq, wk, wv, rope_mla, seq, mla_heads)
    y_c = _mla_attention(mq, mk, mv, batch, seq, mla_heads)

    y_d = _diff_attention(h_rope_h, 0, diff_heads, h_plain, 3 * sb_heads // 2, diff_lambda, diff_subln,
                          lam_init, batch, seq, diff_heads)

    mix = jnp.concatenate([y_a, y_b, y_c, y_d], axis=-1).reshape(batch * seq, -1)
    xf, xb = _mm_ln(mix, _to_bf16(*w_o), xf, ln_attn_g, ln_attn_b, alpha)

    u = _mm_ws_relu2(xb, *w_ff1)
    xf, xb = _mm_ln(u, _to_bf16(*w_ff2), xf, ln_ff_g, ln_ff_b, alpha)

    xf, xb = _mm_ln(xb, _to_bf16(*w_ple_gate), xf, ln_ple_g, ln_ple_b, alpha,
                    ple=(p_i.astype(BF16), w_ple_proj.astype(BF16)), emit_bf16=not last)
    return xf, xb


def kernel(x, p, w_in, w_o, mla_q_norm, mla_w_uq, mla_kv_norm, mla_w_ukv, diff_lambda, diff_subln,
           ln_attn_g, ln_attn_b, w_ff1, w_ff2, ln_ff_g, ln_ff_b, w_ple_gate, w_ple_proj, ln_ple_g, ln_ple_b):
    batch, seq, d_model = x.shape
    depth = w_in.shape[0]
    alpha = (2 * depth) ** 0.25
    tabs = _rope_tables(seq)
    xf = x.reshape(batch * seq, d_model)
    xb = xf.astype(BF16)
    for i in range(depth):
        lam_init = 0.8 - 0.6 * math.exp(-0.3 * i)
        xf, xb = _layer(xf, xb, p[i].reshape(batch * seq, -1), lam_init, alpha, tabs, (batch, seq, d_model),
                        w_in[i], (w_o, i), mla_q_norm[i], mla_w_uq[i], mla_kv_norm[i], mla_w_ukv[i],
                        diff_lambda[i], diff_subln[i], ln_attn_g[i], ln_attn_b[i], (w_ff1, i), (w_ff2, i),
                        ln_ff_g[i], ln_ff_b[i], (w_ple_gate, i), w_ple_proj[i], ln_ple_g[i], ln_ple_b[i],
                        last=i == depth - 1)
    return xf.reshape(batch, seq, d_model)
```
